```python
import math
import jax, jax.numpy as jnp
from jax import lax
import numpy as np

D_MODEL = 1024
BATCH = 32
SEQ = 2048
DEPTH = 4

HEAD_DIM = 64
MLA_HEADS = 8
MLA_NOPE = 64
MLA_ROPE = 32
MLA_V = 64
MLA_Q_LORA = 384
MLA_KV_LORA = 256
ROPE_THETA = 10000.0
SWA_HEADS = 8
SWA_KV_HEADS = 2
SWA_WINDOW = 128
REL_BUCKETS = 32
REL_MAX_DIST = 128
FOX_HEADS = 16
D_FF = 4 * D_MODEL
D_PLE = 256
BLOCK_Q = 128
DN_ALPHA = (2 * DEPTH) ** 0.25
DN_BETA = (8 * DEPTH) ** -0.25
NORM_EPS = 1e-5
NEG_INF = -1e30
N_EVEN = (DEPTH + 1) // 2
N_ODD = DEPTH // 2
EVEN_SPLIT = (MLA_Q_LORA, MLA_KV_LORA, MLA_ROPE, SWA_HEADS * HEAD_DIM,
              SWA_KV_HEADS * HEAD_DIM, SWA_KV_HEADS * HEAD_DIM)
EVEN_IN = MLA_Q_LORA + MLA_KV_LORA + MLA_ROPE + (SWA_HEADS + 2 * SWA_KV_HEADS) * HEAD_DIM
EVEN_MIX = MLA_HEADS * MLA_V + SWA_HEADS * HEAD_DIM
ODD_SPLIT = (FOX_HEADS * HEAD_DIM, FOX_HEADS * HEAD_DIM, FOX_HEADS * HEAD_DIM, FOX_HEADS)
ODD_IN = 3 * FOX_HEADS * HEAD_DIM + FOX_HEADS
ODD_MIX = FOX_HEADS * HEAD_DIM

kernel_name = "hybrid_mla_swa_fox_deepnorm"


def _split(h, sizes):
    out, o = [], 0
    for n in sizes:
        out.append(h[..., o:o + n])
        o += n
    return out


def _layer_norm(x, g, b):
    xf = x.astype(jnp.float32)
    mu = jnp.mean(xf, -1, keepdims=True)
    var = jnp.mean(jnp.square(xf - mu), -1, keepdims=True)
    y = (xf - mu) * lax.rsqrt(var + NORM_EPS)
    return (y * g.astype(jnp.float32) + b.astype(jnp.float32)).astype(x.dtype)


def _rms_norm(x, g):
    xf = x.astype(jnp.float32)
    y = xf * lax.rsqrt(jnp.mean(jnp.square(xf), -1, keepdims=True) + NORM_EPS)
    return (y * g.astype(jnp.float32)).astype(x.dtype)


def _rope_tables(seq_len, dim):
    inv = 1.0 / (ROPE_THETA ** (jnp.arange(0, dim, 2, dtype=jnp.float32) / dim))
    ang = jnp.arange(seq_len, dtype=jnp.float32)[:, None] * inv[None, :]
    return jnp.cos(ang), jnp.sin(ang)


def _apply_rope(x, cos, sin):
    x1, x2 = jnp.split(x.astype(jnp.float32), 2, axis=-1)
    c = cos[:, None, :]
    s = sin[:, None, :]
    return jnp.concatenate([x1 * c - x2 * s, x2 * c + x1 * s], -1).astype(x.dtype)


def _t5_bucket(dist):
    exact = REL_BUCKETS // 2
    d = jnp.maximum(dist, 1).astype(jnp.float32)
    large = exact + (jnp.log(d / exact) / math.log(REL_MAX_DIST / exact)
                     * (REL_BUCKETS - exact)).astype(jnp.int32)
    large = jnp.minimum(large, REL_BUCKETS - 1)
    return jnp.where(dist < exact, dist, large)


def _mla_attend(q_nope, q_rope, k_nope, k_rope, v):
    B, S, H, _ = q_nope.shape
    nb = S // BLOCK_Q
    scale = (MLA_NOPE + MLA_ROPE) ** -0.5
    qn = q_nope.reshape(B, nb, BLOCK_Q, H, MLA_NOPE).transpose(1, 0, 2, 3, 4)
    qr = q_rope.reshape(B, nb, BLOCK_Q, H, MLA_ROPE).transpose(1, 0, 2, 3, 4)
    kpos = jnp.arange(S)

    def block(args):
        i, qn_b, qr_b = args
        s = (jnp.einsum('bqhd,bkhd->bhqk', qn_b, k_nope, preferred_element_type=jnp.float32)
             + jnp.einsum('bqhd,bkd->bhqk', qr_b, k_rope, preferred_element_type=jnp.float32)) * scale
        qpos = i * BLOCK_Q + jnp.arange(BLOCK_Q)
        s = jnp.where(kpos[None, :] <= qpos[:, None], s, NEG_INF)
        w = jax.nn.softmax(s, axis=-1).astype(v.dtype)
        return jnp.einsum('bhqk,bkhd->bqhd', w, v)

    out = lax.map(block, (jnp.arange(nb), qn, qr))
    return out.transpose(1, 0, 2, 3, 4).reshape(B, S, H * MLA_V)


def _swa_attend(q, k, v, sinks, rel_bias):
    B, S, H, d = q.shape
    KVH = k.shape[2]
    G = H // KVH
    nb = S // BLOCK_Q
    qb = q.reshape(B, nb, BLOCK_Q, KVH, G, d)

    def band(t):
        tb = t.reshape(B, nb, BLOCK_Q, KVH, d)
        prev = jnp.pad(tb, ((0, 0), (1, 0), (0, 0), (0, 0), (0, 0)))[:, :-1]
        return jnp.concatenate([prev, tb], axis=2)

    kb, vb = band(k), band(v)
    s = jnp.einsum('bnqkgd,bnskd->bnkgqs', qb, kb, preferred_element_type=jnp.float32) * (d ** -0.5)
    a = jnp.arange(BLOCK_Q)[:, None]
    col = jnp.arange(2 * BLOCK_Q)[None, :]
    dist = a + BLOCK_Q - col
    in_win = (dist >= 0) & (dist < SWA_WINDOW)
    pad = (jnp.arange(nb)[:, None, None] == 0) & (col < BLOCK_Q)[None]
    valid = in_win[None] & ~pad
    bias = rel_bias[_t5_bucket(jnp.maximum(dist, 0))].astype(jnp.float32)
    bias = bias.transpose(2, 0, 1).reshape(KVH, G, BLOCK_Q, 2 * BLOCK_Q)
    s = jnp.where(valid[None, :, None, None], s + bias, NEG_INF)
    sink = jnp.broadcast_to(sinks.astype(jnp.float32).reshape(1, 1, KVH, G, 1, 1), s.shape[:-1] + (1,))
    w = jax.nn.softmax(jnp.concatenate([s, sink], axis=-1), axis=-1)[..., :-1].astype(v.dtype)
    out = jnp.einsum('bnkgqs,bnskd->bnqkgd', w, vb)
    return out.reshape(B, S, H * d)


def _fox_attend(q, k, v, log_f):
    B, S, H, d = q.shape
    nb = S // BLOCK_Q
    c = jnp.cumsum(log_f, axis=1)
    cq = c.reshape(B, nb, BLOCK_Q, H).transpose(1, 0, 3, 2)
    ck = c.transpose(0, 2, 1)
    qb = q.reshape(B, nb, BLOCK_Q, H, d).transpose(1, 0, 2, 3, 4)
    kpos = jnp.arange(S)

    def block(args):
        i, q_b, cq_b = args
        s = jnp.einsum('bqhd,bkhd->bhqk', q_b, k, preferred_element_type=jnp.float32) * (d ** -0.5)
        s = s + cq_b[..., :, None] - ck[:, :, None, :]
        qpos = i * BLOCK_Q + jnp.arange(BLOCK_Q)
        s = jnp.where(kpos[None, :] <= qpos[:, None], s, NEG_INF)
        w = jax.nn.softmax(s, axis=-1).astype(v.dtype)
        return jnp.einsum('bhqk,bkhd->bqhd', w, v)

    out = lax.map(block, (jnp.arange(nb), qb, cq))
    return out.transpose(1, 0, 2, 3, 4).reshape(B, S, H * d)


def _even_mixer(x, w_in, q_norm, w_uq, kv_norm, w_ukv, sinks, w_out, rel_bias, cos, sin):
    B, S, _ = x.shape
    h = x @ w_in
    c_q, c_kv, k_rope, q_s, k_s, v_s = _split(h, EVEN_SPLIT)
    q = (_rms_norm(c_q, q_norm) @ w_uq).reshape(B, S, MLA_HEADS, MLA_NOPE + MLA_ROPE)
    q_nope = q[..., :MLA_NOPE]
    q_rope = _apply_rope(q[..., MLA_NOPE:], cos, sin)
    kv = (_rms_norm(c_kv, kv_norm) @ w_ukv).reshape(B, S, MLA_HEADS, MLA_NOPE + MLA_V)
    k_nope, v = kv[..., :MLA_NOPE], kv[..., MLA_NOPE:]
    k_rope = _apply_rope(k_rope[:, :, None, :], cos, sin)[:, :, 0]
    o_mla = _mla_attend(q_nope, q_rope, k_nope, k_rope, v)
    o_swa = _swa_attend(q_s.reshape(B, S, SWA_HEADS, HEAD_DIM),
                        k_s.reshape(B, S, SWA_KV_HEADS, HEAD_DIM),
                        v_s.reshape(B, S, SWA_KV_HEADS, HEAD_DIM), sinks, rel_bias)
    return jnp.concatenate([o_mla, o_swa], axis=-1) @ w_out


def _odd_mixer(x, w_in, b_f, w_out):
    B, S, _ = x.shape
    q, k, v, f = _split(x @ w_in, ODD_SPLIT)
    log_f = jax.nn.log_sigmoid((f + b_f).astype(jnp.float32))
    o = _fox_attend(q.reshape(B, S, FOX_HEADS, HEAD_DIM), k.reshape(B, S, FOX_HEADS, HEAD_DIM),
                    v.reshape(B, S, FOX_HEADS, HEAD_DIM), log_f)
    return o @ w_out


def _sq_relu_mlp(x, w_up, w_down):
    return jnp.square(jax.nn.relu(x @ w_up)) @ w_down


def setup_inputs(seed: int = 0) -> dict:
    key = jax.random.key(seed)
    ks = jax.random.split(key, 24)
    nrm = jax.random.normal
    f32 = jnp.float32
    return {
        "x": nrm(ks[0], (BATCH, SEQ, D_MODEL), f32),
        "p": nrm(ks[1], (DEPTH, BATCH, SEQ, D_PLE), f32),
        "rel_bias": 0.5 * nrm(ks[2], (REL_BUCKETS, SWA_HEADS), f32),
        "ev_w_in": nrm(ks[3], (N_EVEN, D_MODEL, EVEN_IN), f32) * D_MODEL ** -0.5,
        "ev_q_norm": 1.0 + 0.02 * nrm(ks[4], (N_EVEN, MLA_Q_LORA), f32),
        "ev_w_uq": nrm(ks[5], (N_EVEN, MLA_Q_LORA, MLA_HEADS * (MLA_NOPE + MLA_ROPE)), f32) * MLA_Q_LORA ** -0.5,
        "ev_kv_norm": 1.0 + 0.02 * nrm(ks[6], (N_EVEN, MLA_KV_LORA), f32),
        "ev_w_ukv": nrm(ks[7], (N_EVEN, MLA_KV_LORA, MLA_HEADS * (MLA_NOPE + MLA_V)), f32) * MLA_KV_LORA ** -0.5,
        "ev_sinks": 0.5 * nrm(ks[8], (N_EVEN, SWA_HEADS), f32),
        "ev_w_out": nrm(ks[9], (N_EVEN, EVEN_MIX, D_MODEL), f32) * (EVEN_MIX ** -0.5 * DN_BETA),
        "od_w_in": nrm(ks[10], (N_ODD, D_MODEL, ODD_IN), f32) * D_MODEL ** -0.5,
        "od_b_f": jax.random.uniform(ks[11], (N_ODD, FOX_HEADS), f32, 1.0, 4.0),
        "od_w_out": nrm(ks[12], (N_ODD, ODD_MIX, D_MODEL), f32) * (ODD_MIX ** -0.5 * DN_BETA),
        "ln1_g": 1.0 + 0.02 * nrm(ks[13], (DEPTH, D_MODEL), f32),
        "ln1_b": 0.02 * nrm(ks[14], (DEPTH, D_MODEL), f32),
        "w_up": nrm(ks[15], (DEPTH, D_MODEL, D_FF), f32) * D_MODEL ** -0.5,
        "w_down": nrm(ks[16], (DEPTH, D_FF, D_MODEL), f32) * (D_FF ** -0.5 * DN_BETA),
        "ln2_g": 1.0 + 0.02 * nrm(ks[17], (DEPTH, D_MODEL), f32),
        "ln2_b": 0.02 * nrm(ks[18], (DEPTH, D_MODEL), f32),
        "ple_w_proj": nrm(ks[19], (DEPTH, D_PLE, D_MODEL), f32) * D_PLE ** -0.5,
        "ple_w_gate": nrm(ks[20], (DEPTH, D_MODEL, D_MODEL), f32) * D_MODEL ** -0.5,
        "ple_b_gate": 0.02 * nrm(ks[21], (DEPTH, D_MODEL), f32),
    }


def reference(x, p, rel_bias, ev_w_in, ev_q_norm, ev_w_uq, ev_kv_norm, ev_w_ukv, ev_sinks, ev_w_out,
              od_w_in, od_b_f, od_w_out, ln1_g, ln1_b, w_up, w_down, ln2_g, ln2_b,
              ple_w_proj, ple_w_gate, ple_b_gate):
    S = x.shape[1]
    cos, sin = _rope_tables(S, MLA_ROPE)
    for i in range(DEPTH):
        j = i // 2
        if i % 2 == 0:
            m = _even_mixer(x, ev_w_in[j], ev_q_norm[j], ev_w_uq[j], ev_kv_norm[j], ev_w_ukv[j],
                            ev_sinks[j], ev_w_out[j], rel_bias, cos, sin)
        else:
            m = _odd_mixer(x, od_w_in[j], od_b_f[j], od_w_out[j])
        x = _layer_norm(DN_ALPHA * x + m, ln1_g[i], ln1_b[i])
        x = _layer_norm(DN_ALPHA * x + _sq_relu_mlp(x, w_up[i], w_down[i]), ln2_g[i], ln2_b[i])
        gate = jax.nn.sigmoid(x @ ple_w_gate[i] + ple_b_gate[i])
        x = x + gate * (p[i] @ ple_w_proj[i])
    return x
```

```python
import functools
import math

import jax
import jax.numpy as jnp
import numpy as np
from jax import lax
from jax.experimental import pallas as pl
from jax.experimental.pallas import tpu as pltpu

D_MODEL = 1024
HEAD_DIM = 64
MLA_HEADS = 8
MLA_NOPE = 64
MLA_ROPE = 32
MLA_V = 64
MLA_Q_LORA = 384
MLA_KV_LORA = 256
ROPE_THETA = 10000.0
SWA_HEADS = 8
SWA_KV_HEADS = 2
SWA_WINDOW = 128
REL_BUCKETS = 32
REL_MAX_DIST = 128
FOX_HEADS = 16
D_FF = 4 * D_MODEL
D_PLE = 256
BLOCK_Q = 128
DEPTH = 4
DN_ALPHA = (2 * DEPTH) ** 0.25
NORM_EPS = 1e-5
NEG_INF = -1e30

LANES = 128
MLA_PAD = 128
VMEM_LIMIT = 52 * 1024 * 1024

F32 = jnp.float32
BF16 = jnp.bfloat16

_E_CQ = 0
_E_CKV = _E_CQ + MLA_Q_LORA
_E_KR = _E_CKV + MLA_KV_LORA
_E_KRR = _E_KR + LANES
_E_QS = _E_KRR + LANES
_E_KS = _E_QS + SWA_HEADS * HEAD_DIM
_E_VS = _E_KS + 2 * SWA_KV_HEADS * HEAD_DIM
_E_END = _E_VS + 2 * SWA_KV_HEADS * HEAD_DIM


def _cparams(n_axes):
    return pltpu.CompilerParams(dimension_semantics=("arbitrary",) * n_axes,
                                vmem_limit_bytes=VMEM_LIMIT)


def _const_spec(shape):
    return pl.BlockSpec(shape, lambda *_: (0,) * len(shape))


def _layer_norm(y, g, b):
    mu = jnp.mean(y, axis=-1, keepdims=True)
    yc = y - mu
    var = jnp.mean(yc * yc, axis=-1, keepdims=True)
    return yc * lax.rsqrt(var + NORM_EPS) * g + b


def _rms_norm(y, g):
    return y * lax.rsqrt(jnp.mean(y * y, axis=-1, keepdims=True) + NORM_EPS) * g


def _even_proj_kernel(x_ref, w_in_ref, qn_ref, kvn_ref, w_uq_ref, w_uqr_ref, w_uk_ref, w_uv_ref,
                      qc_ref, qs_ref, kc_ref, ks_ref,
                      qm_ref, km_ref, vm_ref, qsw_ref, ksw_ref, vsw_ref):
    xb = x_ref[...].astype(BF16)
    h = jnp.dot(xb, w_in_ref[...], preferred_element_type=F32)
    cq = _rms_norm(h[:, _E_CQ:_E_CKV], qn_ref[...]).astype(BF16)
    ckv = _rms_norm(h[:, _E_CKV:_E_KR], kvn_ref[...]).astype(BF16)
    qa = jnp.dot(cq, w_uq_ref[...], preferred_element_type=F32)
    qb = jnp.dot(cq, w_uqr_ref[...], preferred_element_type=F32)
    qm_ref[...] = (qa * qc_ref[...] + qb * qs_ref[...]).astype(BF16)
    kr = h[:, _E_KR:_E_KRR] * kc_ref[...] + h[:, _E_KRR:_E_QS] * ks_ref[...]
    kn = jnp.dot(ckv, w_uk_ref[...], preferred_element_type=F32)
    for hd in range(MLA_HEADS):
        sl = slice(hd * MLA_PAD, (hd + 1) * MLA_PAD)
        km_ref[:, sl] = (kn[:, sl] + kr).astype(BF16)
    vm_ref[...] = jnp.dot(ckv, w_uv_ref[...], preferred_element_type=F32).astype(BF16)
    qsw_ref[...] = h[:, _E_QS:_E_KS].astype(BF16)
    ksw_ref[...] = h[:, _E_KS:_E_VS].astype(BF16)
    vsw_ref[...] = h[:, _E_VS:_E_END].astype(BF16)


def _even_proj(x2, w, tabs, seq, tm):
    m = x2.shape[0]
    n_pos = seq // tm
    row = lambda n: pl.BlockSpec((tm, n), lambda i: (i, 0))
    pos = lambda n: pl.BlockSpec((tm, n), lambda i: (i % n_pos, 0))
    hq = MLA_HEADS * MLA_PAD
    out_shape = (
        jax.ShapeDtypeStruct((m, hq), BF16), jax.ShapeDtypeStruct((m, hq), BF16),
        jax.ShapeDtypeStruct((m, MLA_HEADS * MLA_V), BF16),
        jax.ShapeDtypeStruct((m, SWA_HEADS * HEAD_DIM), BF16),
        jax.ShapeDtypeStruct((m, 2 * SWA_KV_HEADS * HEAD_DIM), BF16),
        jax.ShapeDtypeStruct((m, 2 * SWA_KV_HEADS * HEAD_DIM), BF16),
    )
    return pl.pallas_call(
        _even_proj_kernel,
        grid=(m // tm,),
        in_specs=[row(D_MODEL), _const_spec(w["w_in"].shape), _const_spec((1, MLA_Q_LORA)),
                  _const_spec((1, MLA_KV_LORA)), _const_spec(w["w_uq"].shape),
                  _const_spec(w["w_uqr"].shape), _const_spec(w["w_uk"].shape),
                  _const_spec(w["w_uv"].shape), pos(hq), pos(hq), pos(LANES), pos(LANES)],
        out_specs=(row(hq), row(hq), row(MLA_HEADS * MLA_V), row(SWA_HEADS * HEAD_DIM),
                   row(2 * SWA_KV_HEADS * HEAD_DIM), row(2 * SWA_KV_HEADS * HEAD_DIM)),
        out_shape=out_shape,
        compiler_params=_cparams(1),
        name="even_proj",
    )(x2, w["w_in"], w["q_norm"], w["kv_norm"], w["w_uq"], w["w_uqr"], w["w_uk"], w["w_uv"],
      tabs["qc"], tabs["qs"], tabs["kc"], tabs["ks"])


def _odd_proj_kernel(x_ref, wq_ref, wk_ref, wv_ref, wf_ref, bf_ref, q_ref, k_ref, v_ref, c_ref,
                     carry_ref, *, n_pos):
    tm = x_ref.shape[0]
    xb = x_ref[...].astype(BF16)
    q_ref[...] = jnp.dot(xb, wq_ref[...], preferred_element_type=F32).astype(BF16)
    k_ref[...] = jnp.dot(xb, wk_ref[...], preferred_element_type=F32).astype(BF16)
    v_ref[...] = jnp.dot(xb, wv_ref[...], preferred_element_type=F32).astype(BF16)
    z = jnp.dot(xb, wf_ref[...], preferred_element_type=F32) + bf_ref[...]
    logf = jnp.minimum(z, 0.0) - jnp.log1p(jnp.exp(-jnp.abs(z)))

    @pl.when(pl.program_id(0) % n_pos == 0)
    def _():
        carry_ref[...] = jnp.zeros_like(carry_ref)

    rows = lax.broadcasted_iota(jnp.int32, (tm, LANES), 0)
    c = logf
    sh = 1
    while sh < tm:
        c = c + jnp.where(rows >= sh, pltpu.roll(c, sh, 0), 0.0)
        sh *= 2
    c = c + carry_ref[0:1, :]
    c_ref[...] = c
    carry_ref[...] = jnp.broadcast_to(c[tm - 1:tm, :], carry_ref.shape)


def _odd_proj(x2, w, seq, tm):
    m = x2.shape[0]
    n_pos = seq // tm
    row = lambda n: pl.BlockSpec((tm, n), lambda i: (i, 0))
    hd = FOX_HEADS * HEAD_DIM
    out_shape = (jax.ShapeDtypeStruct((m, hd), BF16),) * 3 + (jax.ShapeDtypeStruct((m, LANES), F32),)
    return pl.pallas_call(
        functools.partial(_odd_proj_kernel, n_pos=n_pos),
        grid=(m // tm,),
        in_specs=[row(D_MODEL), _const_spec((D_MODEL, hd)), _const_spec((D_MODEL, hd)),
                  _const_spec((D_MODEL, hd)), _const_spec((D_MODEL, LANES)), _const_spec((1, LANES))],
        out_specs=(row(hd), row(hd), row(hd), row(LANES)),
        out_shape=out_shape,
        scratch_shapes=[pltpu.VMEM((8, LANES), F32)],
        compiler_params=_cparams(1),
        name="odd_proj",
    )(x2, w["wq"], w["wk"], w["wv"], w["wf"], w["bf"])


def _flash_kernel(*refs, tile, seq, dqk, fox):
    if fox:
        q_ref, k_ref, v_ref, cq_ref, ck_ref, o_ref, m_ref, l_ref, acc_ref = refs
    else:
        q_ref, k_ref, v_ref, o_ref, m_ref, l_ref, acc_ref = refs
    nq = seq // tile
    hp = pl.program_id(1)
    lane = lax.broadcasted_iota(jnp.int32, (1, LANES), 1)
    lo = lane < HEAD_DIM
    rowi = lax.broadcasted_iota(jnp.int32, (tile, tile), 0)
    coli = lax.broadcasted_iota(jnp.int32, (tile, tile), 1)
    causal = coli <= rowi
    reps = tile // LANES

    def q_block(qi, carry):
        q0 = pl.multiple_of(qi * tile, tile)
        q_pair = q_ref[0, pl.ds(q0, tile), :]
        if dqk == 2 * LANES:
            qh = (q_pair[:, :LANES], q_pair[:, LANES:])
        else:
            zero = jnp.zeros_like(q_pair)
            qh = (jnp.where(lo, q_pair, zero), jnp.where(lo, zero, q_pair))
        if fox:
            cq_blk = cq_ref[0, pl.ds(q0, tile), :]
            cqh = tuple(jnp.sum(jnp.where(lane == 2 * hp + hh, cq_blk, 0.0), axis=1, keepdims=True)
                        for hh in range(2))
        m_ref[...] = jnp.full(m_ref.shape, NEG_INF, F32)
        l_ref[...] = jnp.zeros(l_ref.shape, F32)
        acc_ref[...] = jnp.zeros(acc_ref.shape, F32)

        def kv_tile(j, masked):
            k0 = pl.multiple_of(j * tile, tile)
            k_pair = k_ref[0, pl.ds(k0, tile), :]
            v_pair = v_ref[0, pl.ds(k0, tile), :]
            vzero = jnp.zeros_like(v_pair)
            pv = None
            alphas = []
            for hh in range(2):
                kh = k_pair[:, hh * LANES:(hh + 1) * LANES] if dqk == 2 * LANES else k_pair
                s = lax.dot_general(qh[hh], kh, (((1,), (1,)), ((), ())),
                                    preferred_element_type=F32)
                if fox:
                    ck_row = ck_ref[0, pl.ds(2 * hp + hh, 1), pl.ds(k0, tile)]
                    s = s + (cqh[hh] - ck_row)
                if masked:
                    s = jnp.where(causal, s, NEG_INF)
                m_prev = m_ref[hh]
                m_new = jnp.maximum(m_prev, jnp.max(s, axis=1, keepdims=True))
                alpha = jnp.exp(m_prev - m_new)
                p = jnp.exp(s - jnp.concatenate([m_new] * reps, axis=1))
                l_ref[hh] = alpha * l_ref[hh] + jnp.sum(p, axis=1, keepdims=True)
                m_ref[hh] = m_new
                vh = jnp.where(lo, v_pair, vzero) if hh == 0 else jnp.where(lo, vzero, v_pair)
                d = jnp.dot(p.astype(BF16), vh, preferred_element_type=F32)
                pv = d if pv is None else pv + d
                alphas.append(alpha)
            acc_ref[...] = acc_ref[...] * jnp.where(lo, alphas[0], alphas[1]) + pv

        def body(j, c):
            kv_tile(j, False)
            return c

        lax.fori_loop(0, qi, body, 0)
        kv_tile(qi, True)
        inv = jnp.where(lo, 1.0 / l_ref[0], 1.0 / l_ref[1])
        o_ref[0, pl.ds(q0, tile), :] = (acc_ref[...] * inv).astype(o_ref.dtype)
        return carry

    lax.fori_loop(0, nq, q_block, 0)


def _flash(q, k, v, cq=None, ck=None, *, tile):
    b, seq, _ = q.shape
    pairs = v.shape[2] // LANES
    dqk = q.shape[2] // pairs
    fox = cq is not None
    blk = lambda n: pl.BlockSpec((1, seq, n), lambda bi, hp: (bi, 0, hp))
    in_specs = [blk(dqk), blk(dqk), blk(LANES)]
    args = [q, k, v]
    if fox:
        in_specs += [pl.BlockSpec((1, seq, LANES), lambda bi, hp: (bi, 0, 0)),
                     pl.BlockSpec((1, ck.shape[1], seq), lambda bi, hp: (bi, 0, 0))]
        args += [cq, ck]
    return pl.pallas_call(
        functools.partial(_flash_kernel, tile=tile, seq=seq, dqk=dqk, fox=fox),
        grid=(b, pairs),
        in_specs=in_specs,
        out_specs=blk(LANES),
        out_shape=jax.ShapeDtypeStruct((b, seq, pairs * LANES), BF16),
        scratch_shapes=[pltpu.VMEM((2, tile, LANES), F32), pltpu.VMEM((2, tile, LANES), F32),
                        pltpu.VMEM((tile, LANES), F32)],
        compiler_params=_cparams(2),
        name="fox_attn" if fox else "mla_attn",
    )(*args)


def _swa_kernel(q_ref, k_ref, v_ref, bias_ref, sink_ref, o_ref, *, seq):
    nb = seq // BLOCK_Q
    lane = lax.broadcasted_iota(jnp.int32, (1, LANES), 1)
    lo = lane < HEAD_DIM
    a = lax.broadcasted_iota(jnp.int32, (BLOCK_Q, BLOCK_Q), 0)
    c = lax.broadcasted_iota(jnp.int32, (BLOCK_Q, BLOCK_Q), 1)
    own_ok = c <= a
    prev_in_win = (a + BLOCK_Q - c) < SWA_WINDOW
    group = SWA_HEADS // SWA_KV_HEADS

    def block(n, carry):
        r0 = pl.multiple_of(n * BLOCK_Q, BLOCK_Q)
        p0 = pl.multiple_of(jnp.maximum(n - 1, 0) * BLOCK_Q, BLOCK_Q)
        prev_ok = jnp.logical_and(prev_in_win, n > 0)
        for pair in range(SWA_HEADS // 2):
            kvh = (2 * pair) // group
            ksl = slice(kvh * LANES, (kvh + 1) * LANES)
            q_pair = q_ref[0, pl.ds(r0, BLOCK_Q), pair * LANES:(pair + 1) * LANES]
            k_own = k_ref[0, pl.ds(r0, BLOCK_Q), ksl]
            k_prev = k_ref[0, pl.ds(p0, BLOCK_Q), ksl]
            v_own = v_ref[0, pl.ds(r0, BLOCK_Q), ksl]
            v_prev = v_ref[0, pl.ds(p0, BLOCK_Q), ksl]
            zq = jnp.zeros_like(q_pair)
            zv = jnp.zeros_like(v_own)
            out = None
            for hh in range(2):
                hd = 2 * pair + hh
                qh = jnp.where(lo, q_pair, zq) if hh == 0 else jnp.where(lo, zq, q_pair)
                nt = (((1,), (1,)), ((), ()))
                s_prev = lax.dot_general(qh, k_prev, nt, preferred_element_type=F32)
                s_own = lax.dot_general(qh, k_own, nt, preferred_element_type=F32)
                s_prev = jnp.where(prev_ok, s_prev + bias_ref[hd, :, :BLOCK_Q], NEG_INF)
                s_own = jnp.where(own_ok, s_own + bias_ref[hd, :, BLOCK_Q:], NEG_INF)
                sink = sink_ref[hd]
                mx = jnp.maximum(jnp.maximum(jnp.max(s_prev, axis=1, keepdims=True),
                                             jnp.max(s_own, axis=1, keepdims=True)), sink)
                p_prev = jnp.exp(s_prev - mx)
                p_own = jnp.exp(s_own - mx)
                den = (jnp.sum(p_prev, axis=1, keepdims=True) + jnp.sum(p_own, axis=1, keepdims=True)
                       + jnp.exp(sink - mx))
                vp = jnp.where(lo, v_prev, zv) if hh == 0 else jnp.where(lo, zv, v_prev)
                vo = jnp.where(lo, v_own, zv) if hh == 0 else jnp.where(lo, zv, v_own)
                d = (jnp.dot(p_prev.astype(BF16), vp, preferred_element_type=F32)
                     + jnp.dot(p_own.astype(BF16), vo, preferred_element_type=F32)) / den
                out = d if out is None else out + d
            o_ref[0, pl.ds(r0, BLOCK_Q), pair * LANES:(pair + 1) * LANES] = out.astype(o_ref.dtype)
        return carry

    lax.fori_loop(0, nb, block, 0)


def _swa(q, k, v, bias, sinks):
    b, seq, dq = q.shape
    dk = k.shape[2]
    return pl.pallas_call(
        functools.partial(_swa_kernel, seq=seq),
        grid=(b,),
        in_specs=[pl.BlockSpec((1, seq, dq), lambda bi: (bi, 0, 0)),
                  pl.BlockSpec((1, seq, dk), lambda bi: (bi, 0, 0)),
                  pl.BlockSpec((1, seq, dk), lambda bi: (bi, 0, 0)),
                  _const_spec(bias.shape),
                  pl.BlockSpec(memory_space=pltpu.SMEM)],
        out_specs=pl.BlockSpec((1, seq, dq), lambda bi: (bi, 0, 0)),
        out_shape=jax.ShapeDtypeStruct((b, seq, dq), BF16),
        compiler_params=_cparams(1),
        name="swa_attn",
    )(q, k, v, bias, sinks)


def _out_proj_kernel(*refs, n_parts):
    o_refs = refs[:n_parts]
    w_refs = refs[n_parts:2 * n_parts]
    x_ref, g_ref, b_ref, y_ref = refs[2 * n_parts:]
    mix = None
    for o_ref, w_ref in zip(o_refs, w_refs):
        d = jnp.dot(o_ref[...], w_ref[...], preferred_element_type=F32)
        mix = d if mix is None else mix + d
    y_ref[...] = _layer_norm(DN_ALPHA * x_ref[...] + mix, g_ref[...], b_ref[...])


def _out_proj(parts, ws, x2, g, b, tm):
    m = x2.shape[0]
    n = len(parts)
    row = lambda c: pl.BlockSpec((tm, c), lambda i: (i, 0))
    return pl.pallas_call(
        functools.partial(_out_proj_kernel, n_parts=n),
        grid=(m // tm,),
        in_specs=[row(p.shape[1]) for p in parts] + [_const_spec(w.shape) for w in ws]
        + [row(D_MODEL), _const_spec((1, D_MODEL)), _const_spec((1, D_MODEL))],
        out_specs=row(D_MODEL),
        out_shape=jax.ShapeDtypeStruct((m, D_MODEL), F32),
        compiler_params=_cparams(1),
        name="out_proj_ln",
    )(*parts, *ws, x2, g, b)


def _mlp_kernel(x_ref, wu_ref, wd_ref, g_ref, b_ref, wg_ref, bg_ref, p_ref, wp_ref, y_ref, acc_ref):
    f = pl.program_id(1)
    xb = x_ref[...].astype(BF16)
    hdn = jnp.maximum(jnp.dot(xb, wu_ref[...], preferred_element_type=F32), 0.0)
    part = jnp.dot((hdn * hdn).astype(BF16), wd_ref[...], preferred_element_type=F32)

    @pl.when(f == 0)
    def _():
        acc_ref[...] = part

    @pl.when(f > 0)
    def _():
        acc_ref[...] += part

    @pl.when(f == pl.num_programs(1) - 1)
    def _():
        y = _layer_norm(DN_ALPHA * x_ref[...] + acc_ref[...], g_ref[...], b_ref[...])
        gate = jax.nn.sigmoid(jnp.dot(y.astype(BF16), wg_ref[...], preferred_element_type=F32)
                              + bg_ref[...])
        emb = jnp.dot(p_ref[...].astype(BF16), wp_ref[...], preferred_element_type=F32)
        y_ref[...] = y + gate * emb


def _mlp(x2, w_up, w_down, g, b, w_gate, b_gate, p2, w_proj, tm, tf):
    m = x2.shape[0]
    row = lambda c: pl.BlockSpec((tm, c), lambda i, f: (i, 0))
    return pl.pallas_call(
        _mlp_kernel,
        grid=(m // tm, D_FF // tf),
        in_specs=[row(D_MODEL),
                  pl.BlockSpec((D_MODEL, tf), lambda i, f: (0, f)),
                  pl.BlockSpec((tf, D_MODEL), lambda i, f: (f, 0)),
                  _const_spec((1, D_MODEL)), _const_spec((1, D_MODEL)),
                  _const_spec((D_MODEL, D_MODEL)), _const_spec((1, D_MODEL)),
                  row(D_PLE), _const_spec((D_PLE, D_MODEL))],
        out_specs=row(D_MODEL),
        out_shape=jax.ShapeDtypeStruct((m, D_MODEL), F32),
        scratch_shapes=[pltpu.VMEM((tm, D_MODEL), F32)],
        compiler_params=_cparams(2),
        name="mlp_ln_ple",
    )(x2, w_up, w_down, g, b, w_gate, b_gate, p2, w_proj)


def _pad_cols(w, n):
    return jnp.pad(w, ((0, 0), (0, n - w.shape[1])))


def _even_weights(w_in, q_norm, w_uq, kv_norm, w_ukv):
    sizes = [MLA_Q_LORA, MLA_KV_LORA, MLA_ROPE, SWA_HEADS * HEAD_DIM, SWA_KV_HEADS * HEAD_DIM]
    c_q, c_kv, k_r, q_s, k_s, v_s = jnp.split(w_in, np.cumsum(sizes).tolist(), axis=1)
    half = MLA_ROPE // 2
    rot = lambda wr: jnp.concatenate([-wr[..., half:], wr[..., :half]], axis=-1)
    d = w_in.shape[0]
    zeros = lambda n: jnp.zeros((d, n), w_in.dtype)
    kr_blk = jnp.concatenate([zeros(MLA_NOPE), k_r, zeros(LANES - MLA_NOPE - MLA_ROPE)], axis=1)
    krr_blk = jnp.concatenate([zeros(MLA_NOPE), rot(k_r), zeros(LANES - MLA_NOPE - MLA_ROPE)], axis=1)
    dup = lambda t: jnp.concatenate(
        [t[:, kv * HEAD_DIM:(kv + 1) * HEAD_DIM] for kv in range(SWA_KV_HEADS) for _ in range(2)], axis=1)
    w_in2 = jnp.concatenate([c_q, c_kv, kr_blk, krr_blk, q_s * HEAD_DIM ** -0.5, dup(k_s), dup(v_s)],
                            axis=1)
    assert w_in2.shape[1] == _E_END
    r = w_uq.shape[0]
    uq = w_uq.reshape(r, MLA_HEADS, MLA_NOPE + MLA_ROPE)
    zq = jnp.zeros((r, MLA_HEADS, MLA_PAD - MLA_NOPE - MLA_ROPE), w_uq.dtype)
    uq_pad = jnp.concatenate([uq, zq], axis=-1).reshape(r, MLA_HEADS * MLA_PAD)
    uq_rot = jnp.concatenate([jnp.zeros_like(uq[..., :MLA_NOPE]), rot(uq[..., MLA_NOPE:]), zq],
                             axis=-1).reshape(r, MLA_HEADS * MLA_PAD)
    rk = w_ukv.shape[0]
    ukv = w_ukv.reshape(rk, MLA_HEADS, MLA_NOPE + MLA_V)
    uk_pad = jnp.concatenate([ukv[..., :MLA_NOPE],
                              jnp.zeros((rk, MLA_HEADS, MLA_PAD - MLA_NOPE), w_ukv.dtype)],
                             axis=-1).reshape(rk, MLA_HEADS * MLA_PAD)
    uv = ukv[..., MLA_NOPE:].reshape(rk, MLA_HEADS * MLA_V)
    return dict(w_in=w_in2.astype(BF16), q_norm=q_norm.reshape(1, -1), kv_norm=kv_norm.reshape(1, -1),
                w_uq=uq_pad.astype(BF16), w_uqr=uq_rot.astype(BF16), w_uk=uk_pad.astype(BF16),
                w_uv=uv.astype(BF16))


def _rope_tables(seq):
    inv = 1.0 / (ROPE_THETA ** (jnp.arange(0, MLA_ROPE, 2, dtype=F32) / MLA_ROPE))
    ang = jnp.arange(seq, dtype=F32)[:, None] * inv[None, :]
    cos = jnp.concatenate([jnp.cos(ang)] * 2, axis=1)
    sin = jnp.concatenate([jnp.sin(ang)] * 2, axis=1)
    ones = jnp.ones((seq, MLA_NOPE), F32)
    z_n = jnp.zeros((seq, MLA_NOPE), F32)
    z_t = jnp.zeros((seq, MLA_PAD - MLA_NOPE - MLA_ROPE), F32)
    kc = jnp.concatenate([z_n, cos, z_t], axis=1)
    ks = jnp.concatenate([z_n, sin, z_t], axis=1)
    scale = (MLA_NOPE + MLA_ROPE) ** -0.5
    qc = jnp.tile(jnp.concatenate([ones, cos, z_t], axis=1) * scale, (1, MLA_HEADS))
    qs = jnp.tile(ks * scale, (1, MLA_HEADS))
    return dict(qc=qc, qs=qs, kc=kc, ks=ks)


def _t5_bucket(dist):
    exact = REL_BUCKETS // 2
    d = jnp.maximum(dist, 1).astype(F32)
    large = exact + (jnp.log(d / exact) / math.log(REL_MAX_DIST / exact)
                     * (REL_BUCKETS - exact)).astype(jnp.int32)
    large = jnp.minimum(large, REL_BUCKETS - 1)
    return jnp.where(dist < exact, dist, large)


def _swa_bias(rel_bias):
    a = jnp.arange(BLOCK_Q)[:, None]
    col = jnp.arange(2 * BLOCK_Q)[None, :]
    dist = a + BLOCK_Q - col
    bias = rel_bias[_t5_bucket(jnp.maximum(dist, 0))].astype(F32)
    return bias.transpose(2, 0, 1)


def kernel(x, p, rel_bias, ev_w_in, ev_q_norm, ev_w_uq, ev_kv_norm, ev_w_ukv, ev_sinks, ev_w_out,
           od_w_in, od_b_f, od_w_out, ln1_g, ln1_b, w_up, w_down, ln2_g, ln2_b,
           ple_w_proj, ple_w_gate, ple_b_gate):
    b, seq, d = x.shape
    m = b * seq
    tm = min(512, seq)
    tile = min(256, seq)
    assert d == D_MODEL and seq % tm == 0 and seq % tile == 0 and seq % BLOCK_Q == 0
    tabs = _rope_tables(seq)
    bias = _swa_bias(rel_bias)
    x2 = x.reshape(m, d)
    hd = FOX_HEADS * HEAD_DIM
    row2 = lambda v: v.reshape(1, -1)
    for i in range(DEPTH):
        j = i // 2
        if i % 2 == 0:
            w = _even_weights(ev_w_in[j], ev_q_norm[j], ev_w_uq[j], ev_kv_norm[j], ev_w_ukv[j])
            qm, km, vm, qsw, ksw, vsw = _even_proj(x2, w, tabs, seq, tm)
            r3 = lambda t: t.reshape(b, seq, t.shape[1])
            o_mla = _flash(r3(qm), r3(km), r3(vm), tile=tile)
            o_swa = _swa(r3(qsw), r3(ksw), r3(vsw), bias, ev_sinks[j])
            w_out = ev_w_out[j].astype(BF16)
            n_mla = MLA_HEADS * MLA_V
            x2 = _out_proj([o_mla.reshape(m, -1), o_swa.reshape(m, -1)],
                           [w_out[:n_mla], w_out[n_mla:]], x2, row2(ln1_g[i]), row2(ln1_b[i]), tm)
        else:
            wi = od_w_in[j]
            w = dict(wq=(wi[:, :hd] * HEAD_DIM ** -0.5).astype(BF16), wk=wi[:, hd:2 * hd].astype(BF16),
                     wv=wi[:, 2 * hd:3 * hd].astype(BF16),
                     wf=_pad_cols(wi[:, 3 * hd:], LANES).astype(BF16),
                     bf=_pad_cols(row2(od_b_f[j]), LANES))
            q, k, v, c = _odd_proj(x2, w, seq, tm)
            r3 = lambda t: t.reshape(b, seq, t.shape[1])
            c3 = r3(c)
            ck = jnp.swapaxes(c3[:, :, :FOX_HEADS], 1, 2)
            o = _flash(r3(q), r3(k), r3(v), c3, ck, tile=tile)
            x2 = _out_proj([o.reshape(m, -1)], [od_w_out[j].astype(BF16)], x2,
                           row2(ln1_g[i]), row2(ln1_b[i]), tm)
        x2 = _mlp(x2, w_up[i].astype(BF16), w_down[i].astype(BF16), row2(ln2_g[i]), row2(ln2_b[i]),
                  ple_w_gate[i].astype(BF16), row2(ple_b_gate[i]), p[i].reshape(m, D_PLE),
                  ple_w_proj[i].astype(BF16), tm, min(1024, D_FF))
    return x2.reshape(b, seq, d)
```

```python
import functools
import math

import jax
import jax.numpy as jnp
import numpy as np
from jax import lax
from jax.experimental import pallas as pl
from jax.experimental.pallas import tpu as pltpu

D_MODEL = 1024
HEAD_DIM = 64
MLA_HEADS = 8
MLA_NOPE = 64
MLA_ROPE = 32
MLA_V = 64
MLA_Q_LORA = 384
MLA_KV_LORA = 256
ROPE_THETA = 10000.0
SWA_HEADS = 8
SWA_KV_HEADS = 2
SWA_WINDOW = 128
REL_BUCKETS = 32
REL_MAX_DIST = 128
FOX_HEADS = 16
D_FF = 4 * D_MODEL
D_PLE = 256
BLOCK_Q = 128
DEPTH = 4
DN_ALPHA = (2 * DEPTH) ** 0.25
NORM_EPS = 1e-5
NEG_INF = -1e30

LANES = 128
MLA_PAD = 128
VMEM_LIMIT = 52 * 1024 * 1024
LOG2E = math.log2(math.e)
DECAY_PARTS = 3
DECAY_LANES = 8

F32 = jnp.float32
BF16 = jnp.bfloat16

_E_CQ = 0
_E_CKV = _E_CQ + MLA_Q_LORA
_E_KR = _E_CKV + MLA_KV_LORA
_E_KRR = _E_KR + LANES
_E_QS = _E_KRR + LANES
_E_KS = _E_QS + SWA_HEADS * HEAD_DIM
_E_VS = _E_KS + 2 * SWA_KV_HEADS * HEAD_DIM
_E_END = _E_VS + 2 * SWA_KV_HEADS * HEAD_DIM


def _cparams(n_axes):
    return pltpu.CompilerParams(dimension_semantics=("arbitrary",) * n_axes,
                                vmem_limit_bytes=VMEM_LIMIT)


def _const_spec(shape):
    return pl.BlockSpec(shape, lambda *_: (0,) * len(shape))


def _layer_norm(y, g, b):
    mu = jnp.mean(y, axis=-1, keepdims=True)
    yc = y - mu
    var = jnp.mean(yc * yc, axis=-1, keepdims=True)
    return yc * lax.rsqrt(var + NORM_EPS) * g + b


def _rms_norm(y, g):
    return y * lax.rsqrt(jnp.mean(y * y, axis=-1, keepdims=True) + NORM_EPS) * g


def _even_proj_kernel(x_ref, w_in_ref, qn_ref, kvn_ref, w_uq_ref, w_uqr_ref, w_uk_ref, w_uv_ref,
                      qc_ref, qs_ref, kc_ref, ks_ref,
                      qm_ref, km_ref, vm_ref, qsw_ref, ksw_ref, vsw_ref):
    xb = x_ref[...].astype(BF16)
    h = jnp.dot(xb, w_in_ref[...], preferred_element_type=F32)
    cq = _rms_norm(h[:, _E_CQ:_E_CKV], qn_ref[...]).astype(BF16)
    ckv = _rms_norm(h[:, _E_CKV:_E_KR], kvn_ref[...]).astype(BF16)
    qa = jnp.dot(cq, w_uq_ref[...], preferred_element_type=F32)
    qb = jnp.dot(cq, w_uqr_ref[...], preferred_element_type=F32)
    qm_ref[...] = (qa * qc_ref[...] + qb * qs_ref[...]).astype(BF16)
    kr = h[:, _E_KR:_E_KRR] * kc_ref[...] + h[:, _E_KRR:_E_QS] * ks_ref[...]
    kn = jnp.dot(ckv, w_uk_ref[...], preferred_element_type=F32)
    for hd in range(MLA_HEADS):
        sl = slice(hd * MLA_PAD, (hd + 1) * MLA_PAD)
        km_ref[:, sl] = (kn[:, sl] + kr).astype(BF16)
    vm_ref[...] = jnp.dot(ckv, w_uv_ref[...], preferred_element_type=F32).astype(BF16)
    qsw_ref[...] = h[:, _E_QS:_E_KS].astype(BF16)
    ksw_ref[...] = h[:, _E_KS:_E_VS].astype(BF16)
    vsw_ref[...] = h[:, _E_VS:_E_END].astype(BF16)


def _even_proj(x2, w, tabs, seq, tm):
    m = x2.shape[0]
    n_pos = seq // tm
    row = lambda n: pl.BlockSpec((tm, n), lambda i: (i, 0))
    pos = lambda n: pl.BlockSpec((tm, n), lambda i: (i % n_pos, 0))
    hq = MLA_HEADS * MLA_PAD
    out_shape = (
        jax.ShapeDtypeStruct((m, hq), BF16), jax.ShapeDtypeStruct((m, hq), BF16),
        jax.ShapeDtypeStruct((m, MLA_HEADS * MLA_V), BF16),
        jax.ShapeDtypeStruct((m, SWA_HEADS * HEAD_DIM), BF16),
        jax.ShapeDtypeStruct((m, 2 * SWA_KV_HEADS * HEAD_DIM), BF16),
        jax.ShapeDtypeStruct((m, 2 * SWA_KV_HEADS * HEAD_DIM), BF16),
    )
    return pl.pallas_call(
        _even_proj_kernel,
        grid=(m // tm,),
        in_specs=[row(D_MODEL), _const_spec(w["w_in"].shape), _const_spec((1, MLA_Q_LORA)),
                  _const_spec((1, MLA_KV_LORA)), _const_spec(w["w_uq"].shape),
                  _const_spec(w["w_uqr"].shape), _const_spec(w["w_uk"].shape),
                  _const_spec(w["w_uv"].shape), pos(hq), pos(hq), pos(LANES), pos(LANES)],
        out_specs=(row(hq), row(hq), row(MLA_HEADS * MLA_V), row(SWA_HEADS * HEAD_DIM),
                   row(2 * SWA_KV_HEADS * HEAD_DIM), row(2 * SWA_KV_HEADS * HEAD_DIM)),
        out_shape=out_shape,
        compiler_params=_cparams(1),
        name="even_proj",
    )(x2, w["w_in"], w["q_norm"], w["kv_norm"], w["w_uq"], w["w_uqr"], w["w_uk"], w["w_uv"],
      tabs["qc"], tabs["qs"], tabs["kc"], tabs["ks"])


def _split3(c):
    hi = c.astype(BF16)
    r1 = c - hi.astype(F32)
    mid = r1.astype(BF16)
    lo = (r1 - mid.astype(F32)).astype(BF16)
    return hi, mid, lo


def _odd_proj_kernel(x_ref, wq_ref, wk_ref, wv_ref, wf_ref, bf_ref, place_ref, ones_ref,
                     q_ref, k_ref, v_ref, qx_ref, kx_ref, carry_ref, *, n_pos):
    tm = x_ref.shape[0]
    xb = x_ref[...].astype(BF16)
    q_ref[...] = jnp.dot(xb, wq_ref[...], preferred_element_type=F32).astype(BF16)
    k_ref[...] = jnp.dot(xb, wk_ref[...], preferred_element_type=F32).astype(BF16)
    v_ref[...] = jnp.dot(xb, wv_ref[...], preferred_element_type=F32).astype(BF16)
    z = jnp.dot(xb, wf_ref[...], preferred_element_type=F32) + bf_ref[...]
    logf = jnp.minimum(z, 0.0) - jnp.log1p(jnp.exp(-jnp.abs(z)))

    @pl.when(pl.program_id(0) % n_pos == 0)
    def _():
        carry_ref[...] = jnp.zeros_like(carry_ref)

    rows = lax.broadcasted_iota(jnp.int32, (tm, LANES), 0)
    c = logf
    sh = 1
    while sh < tm:
        c = c + jnp.where(rows >= sh, pltpu.roll(c, sh, 0), 0.0)
        sh *= 2
    c = c + carry_ref[0:1, :]
    carry_ref[...] = jnp.broadcast_to(c[tm - 1:tm, :], carry_ref.shape)
    parts = _split3(c * LOG2E)
    qx = ones_ref[0:1, :]
    kx = ones_ref[1:2, :]
    for t in range(3):
        qx = qx + jnp.dot(parts[t], place_ref[t], preferred_element_type=F32)
        kx = kx + jnp.dot(parts[t], place_ref[3 + t], preferred_element_type=F32)
    qx_ref[...] = qx.astype(BF16)
    kx_ref[...] = kx.astype(BF16)


def _odd_proj(x2, w, seq, tm):
    m = x2.shape[0]
    n_pos = seq // tm
    row = lambda n: pl.BlockSpec((tm, n), lambda i: (i, 0))
    hd = FOX_HEADS * HEAD_DIM
    out_shape = (jax.ShapeDtypeStruct((m, hd), BF16),) * 3 + (jax.ShapeDtypeStruct((m, LANES), BF16),) * 2
    place, ones = _decay_placement()
    return pl.pallas_call(
        functools.partial(_odd_proj_kernel, n_pos=n_pos),
        grid=(m // tm,),
        in_specs=[row(D_MODEL), _const_spec((D_MODEL, hd)), _const_spec((D_MODEL, hd)),
                  _const_spec((D_MODEL, hd)), _const_spec((D_MODEL, LANES)), _const_spec((1, LANES)),
                  _const_spec(place.shape), _const_spec(ones.shape)],
        out_specs=(row(hd), row(hd), row(hd), row(LANES), row(LANES)),
        out_shape=out_shape,
        scratch_shapes=[pltpu.VMEM((8, LANES), F32)],
        compiler_params=_cparams(1),
        name="odd_proj",
    )(x2, w["wq"], w["wk"], w["wv"], w["wf"], w["bf"], place, ones)


def _decay_placement():
    place = np.zeros((2 * DECAY_PARTS, LANES, LANES), np.float32)
    ones = np.zeros((2, LANES), np.float32)
    for h in range(FOX_HEADS):
        for t in range(DECAY_PARTS):
            place[t, h, DECAY_LANES * h + t] = 1.0
            place[DECAY_PARTS + t, h, DECAY_LANES * h + DECAY_PARTS + t] = -1.0
            ones[0, DECAY_LANES * h + DECAY_PARTS + t] = 1.0
            ones[1, DECAY_LANES * h + t] = 1.0
    return jnp.asarray(place, BF16), jnp.asarray(ones, F32)


def _flash_kernel(*refs, tile, seq, dqk, fox):
    if fox:
        q_ref, k_ref, v_ref, qx_ref, kx_ref, o_ref, s_ref, mc_ref, m_ref, acc_ref = refs
    else:
        q_ref, k_ref, v_ref, o_ref, s_ref, mc_ref, m_ref, acc_ref = refs
    nq = seq // tile
    hp = pl.program_id(1)
    lane = lax.broadcasted_iota(jnp.int32, (1, LANES), 1)
    lo = lane < HEAD_DIM
    rowi = lax.broadcasted_iota(jnp.int32, (tile, tile), 0)
    coli = lax.broadcasted_iota(jnp.int32, (tile, tile), 1)
    causal = coli <= rowi
    reps = tile // LANES
    ones_v = jnp.ones((tile, LANES), BF16)
    nt = (((1,), (1,)), ((), ()))

    def q_block(qi, carry):
        q0 = pl.multiple_of(qi * tile, tile)
        q_pair = q_ref[0, pl.ds(q0, tile), :]
        if fox:
            zero = jnp.zeros_like(q_pair)
            qx_blk = qx_ref[0, pl.ds(q0, tile), :]
            qa = []
            for hh in range(2):
                qm = jnp.where(lo, q_pair, zero) if hh == 0 else jnp.where(lo, zero, q_pair)
                qx = jnp.where(lane // DECAY_LANES == 2 * hp + hh, qx_blk, jnp.zeros_like(qx_blk))
                qa.append(jnp.concatenate([qm, qx], axis=1))
        else:
            qa = [q_pair[:, :LANES], q_pair[:, LANES:]]
        m_ref[...] = jnp.full(m_ref.shape, NEG_INF, F32)
        acc_ref[...] = jnp.zeros(acc_ref.shape, F32)

        def stage_a(j, mask):
            k0 = pl.multiple_of(j * tile, tile)
            k_pair = k_ref[0, pl.ds(k0, tile), :]
            if fox:
                k_pair = jnp.concatenate([k_pair, kx_ref[0, pl.ds(k0, tile), :]], axis=1)
            for hh in range(2):
                kh = k_pair if fox else k_pair[:, hh * LANES:(hh + 1) * LANES]
                s = lax.dot_general(qa[hh], kh, nt, preferred_element_type=F32)
                if mask is not None:
                    s = jnp.where(mask, s, NEG_INF)
                s_ref[hh] = s
                mc_ref[hh] = jnp.broadcast_to(jnp.max(s, axis=1, keepdims=True), (tile, LANES))

        def stage_b(j):
            k0 = pl.multiple_of(j * tile, tile)
            va = jnp.concatenate([v_ref[0, pl.ds(k0, tile), :], ones_v], axis=1)
            for hh in range(2):
                m_prev = m_ref[hh]
                m_new = jnp.maximum(m_prev, mc_ref[hh])
                alpha = jnp.exp2(m_prev - m_new)
                p = jnp.exp2(s_ref[hh] - jnp.concatenate([m_new] * reps, axis=1)).astype(BF16)
                pv = jnp.dot(p, va, preferred_element_type=F32)
                acc_ref[hh] = acc_ref[hh] * jnp.concatenate([alpha, alpha], axis=1) + pv
                m_ref[hh] = m_new

        stage_a(0, jnp.logical_or(causal, qi > 0))

        def body(j, c):
            stage_b(j)
            stage_a(j + 1, None)
            return c

        lax.fori_loop(0, qi - 1, body, 0)

        @pl.when(qi >= 1)
        def _():
            stage_b(qi - 1)
            stage_a(qi, causal)

        stage_b(qi)
        a0 = acc_ref[0]
        a1 = acc_ref[1]
        out = jnp.where(lo, a0[:, :LANES] / a0[:, LANES:], a1[:, :LANES] / a1[:, LANES:])
        o_ref[0, pl.ds(q0, tile), :] = out.astype(o_ref.dtype)
        return carry

    lax.fori_loop(0, nq, q_block, 0)


def _flash(q, k, v, qx=None, kx=None, *, tile):
    b, seq, _ = q.shape
    pairs = v.shape[2] // LANES
    dqk = q.shape[2] // pairs
    fox = qx is not None
    blk = lambda n: pl.BlockSpec((1, seq, n), lambda bi, hp: (bi, 0, hp))
    in_specs = [blk(dqk), blk(dqk), blk(LANES)]
    args = [q, k, v]
    if fox:
        in_specs += [pl.BlockSpec((1, seq, LANES), lambda bi, hp: (bi, 0, 0))] * 2
        args += [qx, kx]
    return pl.pallas_call(
        functools.partial(_flash_kernel, tile=tile, seq=seq, dqk=dqk, fox=fox),
        grid=(b, pairs),
        in_specs=in_specs,
        out_specs=blk(LANES),
        out_shape=jax.ShapeDtypeStruct((b, seq, pairs * LANES), BF16),
        scratch_shapes=[pltpu.VMEM((2, tile, tile), F32), pltpu.VMEM((2, tile, LANES), F32),
                        pltpu.VMEM((2, tile, LANES), F32), pltpu.VMEM((2, tile, 2 * LANES), F32)],
        compiler_params=_cparams(2),
        name="fox_attn" if fox else "mla_attn",
    )(*args)


def _swa_kernel(q_ref, k_ref, v_ref, bias_ref, sink_ref, o_ref, *, seq):
    nb = seq // BLOCK_Q
    lane = lax.broadcasted_iota(jnp.int32, (1, LANES), 1)
    lo = lane < HEAD_DIM
    group = SWA_HEADS // SWA_KV_HEADS
    band = 2 * BLOCK_Q
    ones_v = jnp.ones((band, LANES), BF16)
    nt = (((1,), (1,)), ((), ()))

    def block(n, carry):
        r0 = pl.multiple_of(n * BLOCK_Q, BLOCK_Q)
        b0 = pl.multiple_of(jnp.maximum(n - 1, 0) * BLOCK_Q, BLOCK_Q)
        tab = jnp.where(n == 0, 1, 0)
        for pair in range(SWA_HEADS // 2):
            kvh = (2 * pair) // group
            ksl = slice(kvh * LANES, (kvh + 1) * LANES)
            q_pair = q_ref[0, pl.ds(r0, BLOCK_Q), pair * LANES:(pair + 1) * LANES]
            k_band = k_ref[0, pl.ds(b0, band), ksl]
            va = jnp.concatenate([v_ref[0, pl.ds(b0, band), ksl], ones_v], axis=1)
            zq = jnp.zeros_like(q_pair)
            res = []
            for hh in range(2):
                hd = 2 * pair + hh
                qh = jnp.where(lo, q_pair, zq) if hh == 0 else jnp.where(lo, zq, q_pair)
                s = lax.dot_general(qh, k_band, nt, preferred_element_type=F32) + bias_ref[tab, hd]
                sink = sink_ref[hd] * LOG2E
                mx = jnp.maximum(jnp.max(s, axis=1, keepdims=True), sink)
                p = jnp.exp2(s - mx).astype(BF16)
                pv = jnp.dot(p, va, preferred_element_type=F32)
                den = pv[:, LANES:] + jnp.exp2(sink - mx)
                res.append(pv[:, :LANES] / den)
            o_ref[0, pl.ds(r0, BLOCK_Q), pair * LANES:(pair + 1) * LANES] = (
                jnp.where(lo, res[0], res[1]).astype(o_ref.dtype))
        return carry

    lax.fori_loop(0, nb, block, 0)


def _swa(q, k, v, bias, sinks):
    b, seq, dq = q.shape
    dk = k.shape[2]
    return pl.pallas_call(
        functools.partial(_swa_kernel, seq=seq),
        grid=(b,),
        in_specs=[pl.BlockSpec((1, seq, dq), lambda bi: (bi, 0, 0)),
                  pl.BlockSpec((1, seq, dk), lambda bi: (bi, 0, 0)),
                  pl.BlockSpec((1, seq, dk), lambda bi: (bi, 0, 0)),
                  _const_spec(bias.shape),
                  pl.BlockSpec(memory_space=pltpu.SMEM)],
        out_specs=pl.BlockSpec((1, seq, dq), lambda bi: (bi, 0, 0)),
        out_shape=jax.ShapeDtypeStruct((b, seq, dq), BF16),
        compiler_params=_cparams(1),
        name="swa_attn",
    )(q, k, v, bias, sinks)


def _out_proj_kernel(*refs, n_parts):
    o_refs = refs[:n_parts]
    w_refs = refs[n_parts:2 * n_parts]
    x_ref, g_ref, b_ref, y_ref = refs[2 * n_parts:]
    mix = None
    for o_ref, w_ref in zip(o_refs, w_refs):
        d = jnp.dot(o_ref[...], w_ref[...], preferred_element_type=F32)
        mix = d if mix is None else mix + d
    y_ref[...] = _layer_norm(DN_ALPHA * x_ref[...] + mix, g_ref[...], b_ref[...])


def _out_proj(parts, ws, x2, g, b, tm):
    m = x2.shape[0]
    n = len(parts)
    row = lambda c: pl.BlockSpec((tm, c), lambda i: (i, 0))
    return pl.pallas_call(
        functools.partial(_out_proj_kernel, n_parts=n),
        grid=(m // tm,),
        in_specs=[row(p.shape[1]) for p in parts] + [_const_spec(w.shape) for w in ws]
        + [row(D_MODEL), _const_spec((1, D_MODEL)), _const_spec((1, D_MODEL))],
        out_specs=row(D_MODEL),
        out_shape=jax.ShapeDtypeStruct((m, D_MODEL), F32),
        compiler_params=_cparams(1),
        name="out_proj_ln",
    )(*parts, *ws, x2, g, b)


def _mlp_kernel(x_ref, wu_ref, wd_ref, g_ref, b_ref, wg_ref, bg_ref, p_ref, wp_ref, y_ref, *, tf):
    x = x_ref[...]
    xb = x.astype(BF16)
    acc = None
    for f in range(D_FF // tf):
        hdn = jnp.maximum(jnp.dot(xb, wu_ref[:, f * tf:(f + 1) * tf], preferred_element_type=F32), 0.0)
        part = jnp.dot((hdn * hdn).astype(BF16), wd_ref[f * tf:(f + 1) * tf, :],
                       preferred_element_type=F32)
        acc = part if acc is None else acc + part
    y = _layer_norm(DN_ALPHA * x + acc, g_ref[...], b_ref[...])
    gate = jax.nn.sigmoid(jnp.dot(y.astype(BF16), wg_ref[...], preferred_element_type=F32) + bg_ref[...])
    emb = jnp.dot(p_ref[...].astype(BF16), wp_ref[...], preferred_element_type=F32)
    y_ref[...] = y + gate * emb


def _resident_spec(shape):
    return pl.BlockSpec(shape, lambda *_: (0,) * len(shape), pipeline_mode=pl.Buffered(1))


def _mlp(x2, w_up, w_down, g, b, w_gate, b_gate, p2, w_proj, tm, tf):
    m = x2.shape[0]
    row = lambda c: pl.BlockSpec((tm, c), lambda i: (i, 0))
    return pl.pallas_call(
        functools.partial(_mlp_kernel, tf=tf),
        grid=(m // tm,),
        in_specs=[row(D_MODEL), _resident_spec((D_MODEL, D_FF)), _resident_spec((D_FF, D_MODEL)),
                  _const_spec((1, D_MODEL)), _const_spec((1, D_MODEL)),
                  _resident_spec((D_MODEL, D_MODEL)), _const_spec((1, D_MODEL)),
                  row(D_PLE), _resident_spec((D_PLE, D_MODEL))],
        out_specs=row(D_MODEL),
        out_shape=jax.ShapeDtypeStruct((m, D_MODEL), F32),
        compiler_params=_cparams(1),
        name="mlp_ln_ple",
    )(x2, w_up, w_down, g, b, w_gate, b_gate, p2, w_proj)


def _pad_cols(w, n):
    return jnp.pad(w, ((0, 0), (0, n - w.shape[1])))


def _even_weights(w_in, q_norm, w_uq, kv_norm, w_ukv):
    sizes = [MLA_Q_LORA, MLA_KV_LORA, MLA_ROPE, SWA_HEADS * HEAD_DIM, SWA_KV_HEADS * HEAD_DIM]
    c_q, c_kv, k_r, q_s, k_s, v_s = jnp.split(w_in, np.cumsum(sizes).tolist(), axis=1)
    half = MLA_ROPE // 2
    rot = lambda wr: jnp.concatenate([-wr[..., half:], wr[..., :half]], axis=-1)
    d = w_in.shape[0]
    zeros = lambda n: jnp.zeros((d, n), w_in.dtype)
    kr_blk = jnp.concatenate([zeros(MLA_NOPE), k_r, zeros(LANES - MLA_NOPE - MLA_ROPE)], axis=1)
    krr_blk = jnp.concatenate([zeros(MLA_NOPE), rot(k_r), zeros(LANES - MLA_NOPE - MLA_ROPE)], axis=1)
    dup = lambda t: jnp.concatenate(
        [t[:, kv * HEAD_DIM:(kv + 1) * HEAD_DIM] for kv in range(SWA_KV_HEADS) for _ in range(2)], axis=1)
    w_in2 = jnp.concatenate([c_q, c_kv, kr_blk, krr_blk, q_s * (HEAD_DIM ** -0.5 * LOG2E), dup(k_s), dup(v_s)],
                            axis=1)
    assert w_in2.shape[1] == _E_END
    r = w_uq.shape[0]
    uq = w_uq.reshape(r, MLA_HEADS, MLA_NOPE + MLA_ROPE)
    zq = jnp.zeros((r, MLA_HEADS, MLA_PAD - MLA_NOPE - MLA_ROPE), w_uq.dtype)
    uq_pad = jnp.concatenate([uq, zq], axis=-1).reshape(r, MLA_HEADS * MLA_PAD)
    uq_rot = jnp.concatenate([jnp.zeros_like(uq[..., :MLA_NOPE]), rot(uq[..., MLA_NOPE:]), zq],
                             axis=-1).reshape(r, MLA_HEADS * MLA_PAD)
    rk = w_ukv.shape[0]
    ukv = w_ukv.reshape(rk, MLA_HEADS, MLA_NOPE + MLA_V)
    uk_pad = jnp.concatenate([ukv[..., :MLA_NOPE],
                              jnp.zeros((rk, MLA_HEADS, MLA_PAD - MLA_NOPE), w_ukv.dtype)],
                             axis=-1).reshape(rk, MLA_HEADS * MLA_PAD)
    uv = ukv[..., MLA_NOPE:].reshape(rk, MLA_HEADS * MLA_V)
    return dict(w_in=w_in2.astype(BF16), q_norm=q_norm.reshape(1, -1), kv_norm=kv_norm.reshape(1, -1),
                w_uq=uq_pad.astype(BF16), w_uqr=uq_rot.astype(BF16), w_uk=uk_pad.astype(BF16),
                w_uv=uv.astype(BF16))


def _rope_tables(seq):
    inv = 1.0 / (ROPE_THETA ** (jnp.arange(0, MLA_ROPE, 2, dtype=F32) / MLA_ROPE))
    ang = jnp.arange(seq, dtype=F32)[:, None] * inv[None, :]
    cos = jnp.concatenate([jnp.cos(ang)] * 2, axis=1)
    sin = jnp.concatenate([jnp.sin(ang)] * 2, axis=1)
    ones = jnp.ones((seq, MLA_NOPE), F32)
    z_n = jnp.zeros((seq, MLA_NOPE), F32)
    z_t = jnp.zeros((seq, MLA_PAD - MLA_NOPE - MLA_ROPE), F32)
    kc = jnp.concatenate([z_n, cos, z_t], axis=1)
    ks = jnp.concatenate([z_n, sin, z_t], axis=1)
    scale = (MLA_NOPE + MLA_ROPE) ** -0.5 * LOG2E
    qc =jnp.tile(jnp.concatenate([ones, cos, z_t], axis=1) * scale, (1, MLA_HEADS))
    qs = jnp.tile(ks * scale, (1, MLA_HEADS))
    return dict(qc=qc, qs=qs, kc=kc, ks=ks)


def _t5_bucket(dist):
    exact = REL_BUCKETS // 2
    d = jnp.maximum(dist, 1).astype(F32)
    large = exact + (jnp.log(d / exact) / math.log(REL_MAX_DIST / exact)
                     * (REL_BUCKETS - exact)).astype(jnp.int32)
    large = jnp.minimum(large, REL_BUCKETS - 1)
    return jnp.where(dist < exact, dist, large)


def _swa_bias(rel_bias):
    a = jnp.arange(BLOCK_Q)[:, None]
    col = jnp.arange(2 * BLOCK_Q)[None, :]
    dist = a + BLOCK_Q - col
    bias = rel_bias[_t5_bucket(jnp.maximum(dist, 0))].astype(F32)
    bias = bias.transpose(2, 0, 1) * LOG2E
    in_win = (dist >= 0) & (dist < SWA_WINDOW)
    main = jnp.where(in_win[None], bias, NEG_INF)
    first = jnp.where((col < BLOCK_Q) & (col <= a), jnp.roll(bias, -BLOCK_Q, axis=2), NEG_INF)
    return jnp.stack([main, jnp.broadcast_to(first, main.shape)])


def kernel(x, p, rel_bias, ev_w_in, ev_q_norm, ev_w_uq, ev_kv_norm, ev_w_ukv, ev_sinks, ev_w_out,
           od_w_in, od_b_f, od_w_out, ln1_g, ln1_b, w_up, w_down, ln2_g, ln2_b,
           ple_w_proj, ple_w_gate, ple_b_gate):
    b, seq, d = x.shape
    m = b * seq
    tm = min(512, seq)
    tile = min(512, seq)
    assert d == D_MODEL and seq % tm == 0 and seq % tile == 0 and seq % BLOCK_Q == 0
    assert seq >= 2 * BLOCK_Q
    tabs = _rope_tables(seq)
    bias = _swa_bias(rel_bias)
    x2 = x.reshape(m, d)
    hd = FOX_HEADS * HEAD_DIM
    row2 = lambda v: v.reshape(1, -1)
    for i in range(DEPTH):
        j = i // 2
        if i % 2 == 0:
            w = _even_weights(ev_w_in[j], ev_q_norm[j], ev_w_uq[j], ev_kv_norm[j], ev_w_ukv[j])
            qm, km, vm, qsw, ksw, vsw = _even_proj(x2, w, tabs, seq, tm)
            r3 = lambda t: t.reshape(b, seq, t.shape[1])
            o_mla = _flash(r3(qm), r3(km), r3(vm), tile=tile)
            o_swa = _swa(r3(qsw), r3(ksw), r3(vsw), bias, ev_sinks[j])
            w_out = ev_w_out[j].astype(BF16)
            n_mla = MLA_HEADS * MLA_V
            x2 = _out_proj([o_mla.reshape(m, -1), o_swa.reshape(m, -1)],
                           [w_out[:n_mla], w_out[n_mla:]], x2, row2(ln1_g[i]), row2(ln1_b[i]), tm)
        else:
            wi = od_w_in[j]
            w = dict(wq=(wi[:, :hd] * (HEAD_DIM ** -0.5 * LOG2E)).astype(BF16),
                     wk=wi[:, hd:2 * hd].astype(BF16),
                     wv=wi[:, 2 * hd:3 * hd].astype(BF16),
                     wf=_pad_cols(wi[:, 3 * hd:], LANES).astype(BF16),
                     bf=_pad_cols(row2(od_b_f[j]), LANES))
            q, k, v, qx, kx = _odd_proj(x2, w, seq, tm)
            r3 = lambda t: t.reshape(b, seq, t.shape[1])
            o = _flash(r3(q), r3(k), r3(v), r3(qx), r3(kx), tile=tile)
            x2 = _out_proj([o.reshape(m, -1)], [od_w_out[j].astype(BF16)], x2,
                           row2(ln1_g[i]), row2(ln1_b[i]), tm)
        x2 = _mlp(x2, w_up[i].astype(BF16), w_down[i].astype(BF16), row2(ln2_g[i]), row2(ln2_b[i]),
                  ple_w_gate[i].astype(BF16), row2(ple_b_gate[i]), p[i].reshape(m, D_PLE),
                  ple_w_proj[i].astype(BF16), tm, min(1024, D_FF))
    return x2.reshape(b, seq, d)
```

```python
import functools
import math

import jax
import jax.numpy as jnp
import numpy as np
from jax import lax
from jax.experimental import pallas as pl
from jax.experimental.pallas import tpu as pltpu

D_MODEL = 1024
HEAD_DIM = 64
MLA_HEADS = 8
MLA_NOPE = 64
MLA_ROPE = 32
MLA_V = 64
MLA_Q_LORA = 384
MLA_KV_LORA = 256
ROPE_THETA = 10000.0
SWA_HEADS = 8
SWA_KV_HEADS = 2
SWA_WINDOW = 128
REL_BUCKETS = 32
REL_MAX_DIST = 128
FOX_HEADS = 16
D_FF = 4 * D_MODEL
D_PLE = 256
BLOCK_Q = 128
DEPTH = 4
DN_ALPHA = (2 * DEPTH) ** 0.25
NORM_EPS = 1e-5
NEG_INF = -1e30

LANES = 128
MLA_PAD = 128
VMEM_LIMIT = 52 * 1024 * 1024
LOG2E = math.log2(math.e)
FLASH_PAIRS_PER_STEP = 4
DECAY_PARTS = 3
DECAY_LANES = 8

F32 = jnp.float32
BF16 = jnp.bfloat16

_E_CQ = 0
_E_CKV = _E_CQ + MLA_Q_LORA
_E_KR = _E_CKV + MLA_KV_LORA
_E_KRR = _E_KR + LANES
_E_QS = _E_KRR + LANES
_E_KS = _E_QS + SWA_HEADS * HEAD_DIM
_E_VS = _E_KS + 2 * SWA_KV_HEADS * HEAD_DIM
_E_END = _E_VS + 2 * SWA_KV_HEADS * HEAD_DIM


def _cparams(n_axes):
    return pltpu.CompilerParams(dimension_semantics=("arbitrary",) * n_axes,
                                vmem_limit_bytes=VMEM_LIMIT)


def _const_spec(shape):
    return pl.BlockSpec(shape, lambda *_: (0,) * len(shape))


def _layer_norm(y, g, b):
    mu = jnp.mean(y, axis=-1, keepdims=True)
    yc = y - mu
    var = jnp.mean(yc * yc, axis=-1, keepdims=True)
    return yc * lax.rsqrt(var + NORM_EPS) * g + b


def _rms_norm(y, g):
    return y * lax.rsqrt(jnp.mean(y * y, axis=-1, keepdims=True) + NORM_EPS) * g


def _even_proj_kernel(x_ref, w_in_ref, qn_ref, kvn_ref, w_uq_ref, w_uqr_ref, w_uk_ref, w_uv_ref,
                      qc_ref, qs_ref, kc_ref, ks_ref,
                      qm_ref, km_ref, vm_ref, qsw_ref, ksw_ref, vsw_ref):
    xb = x_ref[...].astype(BF16)
    h = jnp.dot(xb, w_in_ref[...], preferred_element_type=F32)
    cq = _rms_norm(h[:, _E_CQ:_E_CKV], qn_ref[...]).astype(BF16)
    ckv = _rms_norm(h[:, _E_CKV:_E_KR], kvn_ref[...]).astype(BF16)
    qa = jnp.dot(cq, w_uq_ref[...], preferred_element_type=F32)
    qb = jnp.dot(cq, w_uqr_ref[...], preferred_element_type=F32)
    qm_ref[...] = (qa * qc_ref[...] + qb * qs_ref[...]).astype(BF16)
    kr = h[:, _E_KR:_E_KRR] * kc_ref[...] + h[:, _E_KRR:_E_QS] * ks_ref[...]
    kn = jnp.dot(ckv, w_uk_ref[...], preferred_element_type=F32)
    for hd in range(MLA_HEADS):
        sl = slice(hd * MLA_PAD, (hd + 1) * MLA_PAD)
        km_ref[:, sl] = (kn[:, sl] + kr).astype(BF16)
    vm_ref[...] = jnp.dot(ckv, w_uv_ref[...], preferred_element_type=F32).astype(BF16)
    qsw_ref[...] = h[:, _E_QS:_E_KS].astype(BF16)
    ksw_ref[...] = h[:, _E_KS:_E_VS].astype(BF16)
    vsw_ref[...] = h[:, _E_VS:_E_END].astype(BF16)


def _even_proj(x2, w, tabs, seq, tm):
    m = x2.shape[0]
    n_pos = seq // tm
    row = lambda n: pl.BlockSpec((tm, n), lambda i: (i, 0))
    pos = lambda n: pl.BlockSpec((tm, n), lambda i: (i % n_pos, 0))
    hq = MLA_HEADS * MLA_PAD
    out_shape = (
        jax.ShapeDtypeStruct((m, hq), BF16), jax.ShapeDtypeStruct((m, hq), BF16),
        jax.ShapeDtypeStruct((m, MLA_HEADS * MLA_V), BF16),
        jax.ShapeDtypeStruct((m, SWA_HEADS * HEAD_DIM), BF16),
        jax.ShapeDtypeStruct((m, 2 * SWA_KV_HEADS * HEAD_DIM), BF16),
        jax.ShapeDtypeStruct((m, 2 * SWA_KV_HEADS * HEAD_DIM), BF16),
    )
    return pl.pallas_call(
        _even_proj_kernel,
        grid=(m // tm,),
        in_specs=[row(D_MODEL), _const_spec(w["w_in"].shape), _const_spec((1, MLA_Q_LORA)),
                  _const_spec((1, MLA_KV_LORA)), _const_spec(w["w_uq"].shape),
                  _const_spec(w["w_uqr"].shape), _const_spec(w["w_uk"].shape),
                  _const_spec(w["w_uv"].shape), pos(hq), pos(hq), pos(LANES), pos(LANES)],
        out_specs=(row(hq), row(hq), row(MLA_HEADS * MLA_V), row(SWA_HEADS * HEAD_DIM),
                   row(2 * SWA_KV_HEADS * HEAD_DIM), row(2 * SWA_KV_HEADS * HEAD_DIM)),
        out_shape=out_shape,
        compiler_params=_cparams(1),
        name="even_proj",
    )(x2, w["w_in"], w["q_norm"], w["kv_norm"], w["w_uq"], w["w_uqr"], w["w_uk"], w["w_uv"],
      tabs["qc"], tabs["qs"], tabs["kc"], tabs["ks"])


def _split3(c):
    hi = c.astype(BF16)
    r1 = c - hi.astype(F32)
    mid = r1.astype(BF16)
    lo = (r1 - mid.astype(F32)).astype(BF16)
    return hi, mid, lo


def _odd_proj_kernel(x_ref, wq_ref, wk_ref, wv_ref, wf_ref, bf_ref, place_ref, ones_ref,
                     q_ref, k_ref, v_ref, qx_ref, kx_ref, carry_ref, *, n_pos):
    tm = x_ref.shape[0]
    xb = x_ref[...].astype(BF16)
    q_ref[...] = jnp.dot(xb, wq_ref[...], preferred_element_type=F32).astype(BF16)
    k_ref[...] = jnp.dot(xb, wk_ref[...], preferred_element_type=F32).astype(BF16)
    v_ref[...] = jnp.dot(xb, wv_ref[...], preferred_element_type=F32).astype(BF16)
    z = jnp.dot(xb, wf_ref[...], preferred_element_type=F32) + bf_ref[...]
    logf = jnp.minimum(z, 0.0) - jnp.log1p(jnp.exp(-jnp.abs(z)))

    @pl.when(pl.program_id(0) % n_pos == 0)
    def _():
        carry_ref[...] = jnp.zeros_like(carry_ref)

    rows = lax.broadcasted_iota(jnp.int32, (tm, LANES), 0)
    c = logf
    sh = 1
    while sh < tm:
        c = c + jnp.where(rows >= sh, pltpu.roll(c, sh, 0), 0.0)
        sh *= 2
    c = c + carry_ref[0:1, :]
    carry_ref[...] = jnp.broadcast_to(c[tm - 1:tm, :], carry_ref.shape)
    parts = jnp.concatenate(_split3(c * LOG2E), axis=1)
    placed = jnp.dot(parts, place_ref[...], preferred_element_type=F32) + ones_ref[...]
    qx_ref[...] = placed[:, :LANES].astype(BF16)
    kx_ref[...] = placed[:, LANES:].astype(BF16)


def _odd_proj(x2, w, seq, tm):
    m = x2.shape[0]
    n_pos = seq // tm
    row = lambda n: pl.BlockSpec((tm, n), lambda i: (i, 0))
    hd = FOX_HEADS * HEAD_DIM
    out_shape = (jax.ShapeDtypeStruct((m, hd), BF16),) * 3 + (jax.ShapeDtypeStruct((m, LANES), BF16),) * 2
    place, ones = _decay_placement()
    return pl.pallas_call(
        functools.partial(_odd_proj_kernel, n_pos=n_pos),
        grid=(m // tm,),
        in_specs=[row(D_MODEL), _const_spec((D_MODEL, hd)), _const_spec((D_MODEL, hd)),
                  _const_spec((D_MODEL, hd)), _const_spec((D_MODEL, LANES)), _const_spec((1, LANES)),
                  _const_spec(place.shape), _const_spec(ones.shape)],
        out_specs=(row(hd), row(hd), row(hd), row(LANES), row(LANES)),
        out_shape=out_shape,
        scratch_shapes=[pltpu.VMEM((8, LANES), F32)],
        compiler_params=_cparams(1),
        name="odd_proj",
    )(x2, w["wq"], w["wk"], w["wv"], w["wf"], w["bf"], place, ones)


def _decay_placement():
    place = np.zeros((DECAY_PARTS * LANES, 2 * LANES), np.float32)
    ones = np.zeros((1, 2 * LANES), np.float32)
    for h in range(FOX_HEADS):
        for t in range(DECAY_PARTS):
            place[t * LANES + h, DECAY_LANES * h + t] = 1.0
            place[t * LANES + h, LANES + DECAY_LANES * h + DECAY_PARTS + t] = -1.0
            ones[0, DECAY_LANES * h + DECAY_PARTS + t] = 1.0
            ones[0, LANES + DECAY_LANES * h + t] = 1.0
    return jnp.asarray(place, BF16), jnp.asarray(ones, F32)


def _flash_kernel(*refs, tile, seq, dqk, fox, pairs):
    if fox:
        q_ref, k_ref, v_ref, qx_ref, kx_ref, o_ref, s_ref, mc_ref, m_ref, acc_ref, mask_ref = refs
    else:
        q_ref, k_ref, v_ref, o_ref, s_ref, mc_ref, m_ref, acc_ref, mask_ref = refs
    nq = seq // tile
    n_steps = nq * (nq + 1) // 2
    first_pair = pl.program_id(1) * pairs
    lane = lax.broadcasted_iota(jnp.int32, (1, LANES), 1)
    lo = lane < HEAD_DIM
    reps = tile // LANES
    nt = (((1,), (1,)), ((), ()))
    lane_full = lax.broadcasted_iota(jnp.int32, (tile, LANES), 1)
    ones_lo = jnp.where(lane_full < HEAD_DIM, 1.0, 0.0).astype(BF16)
    ones_hi = jnp.where(lane_full < HEAD_DIM, 0.0, 1.0).astype(BF16)

    def q_operands(qi):
        q0 = pl.multiple_of(qi * tile, tile)
        qa = []
        for pp in range(pairs):
            q_pair = q_ref[0, pl.ds(q0, tile), pp * dqk:(pp + 1) * dqk]
            if fox:
                zero = jnp.zeros_like(q_pair)
                qx_blk = qx_ref[0, pl.ds(q0, tile), :]
                for hh in range(2):
                    qm = jnp.where(lo, q_pair, zero) if hh == 0 else jnp.where(lo, zero, q_pair)
                    head = 2 * (first_pair + pp) + hh
                    qx = jnp.where(lane // DECAY_LANES == head, qx_blk, jnp.zeros_like(qx_blk))
                    qa.append(jnp.concatenate([qm, qx], axis=1))
            else:
                qa += [q_pair[:, :LANES], q_pair[:, LANES:]]
        return qa

    def stage_a(qi, j):
        qa = q_operands(qi)
        k0 = pl.multiple_of(j * tile, tile)
        mask = mask_ref[jnp.where(j == qi, 1, 0)]
        for pp in range(pairs):
            k_pair = k_ref[0, pl.ds(k0, tile), pp * dqk:(pp + 1) * dqk]
            if fox:
                k_pair = jnp.concatenate([k_pair, kx_ref[0, pl.ds(k0, tile), :]], axis=1)
            for hh in range(2):
                h = 2 * pp + hh
                kh = k_pair if fox else k_pair[:, hh * LANES:(hh + 1) * LANES]
                s = lax.dot_general(qa[h], kh, nt, preferred_element_type=F32)
                s = s + mask
                s_ref[h] = s
                mc_ref[h] = jnp.broadcast_to(jnp.max(s, axis=1, keepdims=True), (tile, LANES))

    def stage_b(j):
        k0 = pl.multiple_of(j * tile, tile)
        m_cap = jnp.where(j == 0, NEG_INF, -NEG_INF)
        for pp in range(pairs):
            v_pair = v_ref[0, pl.ds(k0, tile), pp * LANES:(pp + 1) * LANES]
            zero = jnp.zeros_like(v_pair)
            v_stack = jnp.concatenate(
                [jnp.concatenate([jnp.where(lo, v_pair, zero), ones_lo], axis=1),
                 jnp.concatenate([jnp.where(lo, zero, v_pair), ones_hi], axis=1)],
                axis=0)
            ps = []
            alphas = []
            for hh in range(2):
                h = 2 * pp + hh
                m_prev = jnp.minimum(m_ref[h], m_cap)
                m_new = jnp.maximum(m_prev, mc_ref[h])
                alphas.append(jnp.exp2(m_prev - m_new))
                ps.append(jnp.exp2(s_ref[h] - jnp.concatenate([m_new] * reps, axis=1)).astype(BF16))
                m_ref[h] = m_new
            pv = jnp.dot(jnp.concatenate(ps, axis=1), v_stack, preferred_element_type=F32)
            alpha = jnp.where(lo, alphas[0], alphas[1])
            acc_ref[pp] = acc_ref[pp] * jnp.concatenate([alpha, alpha], axis=1) + pv

    def finalize(qi):
        q0 = pl.multiple_of(qi * tile, tile)
        for pp in range(pairs):
            a = acc_ref[pp]
            o_ref[0, pl.ds(q0, tile), pp * LANES:(pp + 1) * LANES] = (
                a[:, :LANES] / a[:, LANES:]).astype(o_ref.dtype)

    m_ref[...] = jnp.zeros(m_ref.shape, F32)
    acc_ref[...] = jnp.zeros(acc_ref.shape, F32)
    dcol = (lax.broadcasted_iota(jnp.int32, (tile, tile), 1)
            - lax.broadcasted_iota(jnp.int32, (tile, tile), 0))
    mask_ref[0] = jnp.zeros((tile, tile), F32)
    mask_ref[1] = jnp.where(dcol <= 0, 0.0, NEG_INF)
    stage_a(0, 0)

    def body(t, carry):
        qi, j = carry
        last = j == qi
        qn = jnp.where(last, qi + 1, qi)
        jn = jnp.where(last, 0, j + 1)
        stage_b(j)
        stage_a(qn, jn)

        @pl.when(last)
        def _():
            finalize(qi)

        return qn, jn

    lax.fori_loop(0, n_steps - 1, body, (jnp.int32(0), jnp.int32(0)))
    stage_b(nq - 1)
    finalize(nq - 1)


def _flash(q, k, v, qx=None, kx=None, *, tile):
    b, seq, _ = q.shape
    all_pairs = v.shape[2] // LANES
    dqk = q.shape[2] // all_pairs
    fox = qx is not None
    pairs = FLASH_PAIRS_PER_STEP
    heads = 2 * pairs
    assert all_pairs % pairs == 0
    blk = lambda n: pl.BlockSpec((1, seq, pairs * n), lambda bi, g: (bi, 0, g))
    in_specs = [blk(dqk), blk(dqk), blk(LANES)]
    args = [q, k, v]
    if fox:
        in_specs += [pl.BlockSpec((1, seq, LANES), lambda bi, g: (bi, 0, 0))] * 2
        args += [qx, kx]
    return pl.pallas_call(
        functools.partial(_flash_kernel, tile=tile, seq=seq, dqk=dqk, fox=fox, pairs=pairs),
        grid=(b, all_pairs // pairs),
        in_specs=in_specs,
        out_specs=blk(LANES),
        out_shape=jax.ShapeDtypeStruct((b, seq, all_pairs * LANES), BF16),
        scratch_shapes=[pltpu.VMEM((heads, tile, tile), F32), pltpu.VMEM((heads, tile, LANES), F32),
                        pltpu.VMEM((heads, tile, LANES), F32), pltpu.VMEM((pairs, tile, 2 * LANES), F32),
                        pltpu.VMEM((2, tile, tile), F32)],
        compiler_params=_cparams(2),
        name="fox_attn" if fox else "mla_attn",
    )(*args)


def _swa_kernel(q_ref, k_ref, v_ref, bias_ref, sink_ref, o_ref, *, seq):
    nb = seq // BLOCK_Q
    lane = lax.broadcasted_iota(jnp.int32, (1, LANES), 1)
    lo = lane < HEAD_DIM
    group = SWA_HEADS // SWA_KV_HEADS
    band = 2 * BLOCK_Q
    ones_v = jnp.ones((band, LANES), BF16)
    nt = (((1,), (1,)), ((), ()))

    def block(n, carry):
        r0 = pl.multiple_of(n * BLOCK_Q, BLOCK_Q)
        b0 = pl.multiple_of(jnp.maximum(n - 1, 0) * BLOCK_Q, BLOCK_Q)
        tab = jnp.where(n == 0, 1, 0)
        for pair in range(SWA_HEADS // 2):
            kvh = (2 * pair) // group
            ksl = slice(kvh * LANES, (kvh + 1) * LANES)
            q_pair = q_ref[0, pl.ds(r0, BLOCK_Q), pair * LANES:(pair + 1) * LANES]
            k_band = k_ref[0, pl.ds(b0, band), ksl]
            va = jnp.concatenate([v_ref[0, pl.ds(b0, band), ksl], ones_v], axis=1)
            zq = jnp.zeros_like(q_pair)
            res = []
            for hh in range(2):
                hd = 2 * pair + hh
                qh = jnp.where(lo, q_pair, zq) if hh == 0 else jnp.where(lo, zq, q_pair)
                s = lax.dot_general(qh, k_band, nt, preferred_element_type=F32) + bias_ref[tab, hd]
                sink = sink_ref[hd] * LOG2E
                mx = jnp.maximum(jnp.max(s, axis=1, keepdims=True), sink)
                p = jnp.exp2(s - mx).astype(BF16)
                pv = jnp.dot(p, va, preferred_element_type=F32)
                den = pv[:, LANES:] + jnp.exp2(sink - mx)
                res.append(pv[:, :LANES] / den)
            o_ref[0, pl.ds(r0, BLOCK_Q), pair * LANES:(pair + 1) * LANES] = (
                jnp.where(lo, res[0], res[1]).astype(o_ref.dtype))
        return carry

    lax.fori_loop(0, nb, block, 0)


def _swa(q, k, v, bias, sinks):
    b, seq, dq = q.shape
    dk = k.shape[2]
    return pl.pallas_call(
        functools.partial(_swa_kernel, seq=seq),
        grid=(b,),
        in_specs=[pl.BlockSpec((1, seq, dq), lambda bi: (bi, 0, 0)),
                  pl.BlockSpec((1, seq, dk), lambda bi: (bi, 0, 0)),
                  pl.BlockSpec((1, seq, dk), lambda bi: (bi, 0, 0)),
                  _const_spec(bias.shape),
                  pl.BlockSpec(memory_space=pltpu.SMEM)],
        out_specs=pl.BlockSpec((1, seq, dq), lambda bi: (bi, 0, 0)),
        out_shape=jax.ShapeDtypeStruct((b, seq, dq), BF16),
        compiler_params=_cparams(1),
        name="swa_attn",
    )(q, k, v, bias, sinks)


def _tail_kernel(*refs, n_parts, tf):
    o_refs = refs[:n_parts]
    w_refs = refs[n_parts:2 * n_parts]
    (x_ref, g1_ref, b1_ref, wu_ref, wd_ref, g2_ref, b2_ref, wg_ref, bg_ref, p_ref, wp_ref,
     y_ref) = refs[2 * n_parts:]
    mix = None
    for o_ref, w_ref in zip(o_refs, w_refs):
        d = jnp.dot(o_ref[...], w_ref[...], preferred_element_type=F32)
        mix = d if mix is None else mix + d
    x1 = _layer_norm(DN_ALPHA * x_ref[...] + mix, g1_ref[...], b1_ref[...])
    xb = x1.astype(BF16)
    acc = None
    for f in range(D_FF // tf):
        hdn = jnp.maximum(jnp.dot(xb, wu_ref[:, f * tf:(f + 1) * tf], preferred_element_type=F32), 0.0)
        part = jnp.dot((hdn * hdn).astype(BF16), wd_ref[f * tf:(f + 1) * tf, :],
                       preferred_element_type=F32)
        acc = part if acc is None else acc + part
    y = _layer_norm(DN_ALPHA * x1 + acc, g2_ref[...], b2_ref[...])
    gate = jax.nn.sigmoid(jnp.dot(y.astype(BF16), wg_ref[...], preferred_element_type=F32) + bg_ref[...])
    emb = jnp.dot(p_ref[...].astype(BF16), wp_ref[...], preferred_element_type=F32)
    y_ref[...] = y + gate * emb


def _resident_spec(shape):
    return pl.BlockSpec(shape, lambda *_: (0,) * len(shape), pipeline_mode=pl.Buffered(1))


def _layer_tail(parts, w_outs, x2, g1, b1, w_up, w_down, g2, b2, w_gate, b_gate, p2, w_proj, tm, tf):
    m = x2.shape[0]
    row = lambda c: pl.BlockSpec((tm, c), lambda i: (i, 0))
    vec = _const_spec((1, D_MODEL))
    return pl.pallas_call(
        functools.partial(_tail_kernel, n_parts=len(parts), tf=tf),
        grid=(m // tm,),
        in_specs=[row(o.shape[1]) for o in parts] + [_resident_spec(w.shape) for w in w_outs]
        + [row(D_MODEL), vec, vec, _resident_spec((D_MODEL, D_FF)), _resident_spec((D_FF, D_MODEL)),
           vec, vec, _resident_spec((D_MODEL, D_MODEL)), vec, row(D_PLE),
           _resident_spec((D_PLE, D_MODEL))],
        out_specs=row(D_MODEL),
        out_shape=jax.ShapeDtypeStruct((m, D_MODEL), F32),
        compiler_params=_cparams(1),
        name="layer_tail",
    )(*parts, *w_outs, x2, g1, b1, w_up, w_down, g2, b2, w_gate, b_gate, p2, w_proj)


def _pad_cols(w, n):
    return jnp.pad(w, ((0, 0), (0, n - w.shape[1])))


def _even_weights(w_in, q_norm, w_uq, kv_norm, w_ukv):
    sizes = [MLA_Q_LORA, MLA_KV_LORA, MLA_ROPE, SWA_HEADS * HEAD_DIM, SWA_KV_HEADS * HEAD_DIM]
    c_q, c_kv, k_r, q_s, k_s, v_s = jnp.split(w_in, np.cumsum(sizes).tolist(), axis=1)
    half = MLA_ROPE // 2
    rot = lambda wr: jnp.concatenate([-wr[..., half:], wr[..., :half]], axis=-1)
    d = w_in.shape[0]
    zeros = lambda n: jnp.zeros((d, n), w_in.dtype)
    kr_blk = jnp.concatenate([zeros(MLA_NOPE), k_r, zeros(LANES - MLA_NOPE - MLA_ROPE)], axis=1)
    krr_blk = jnp.concatenate([zeros(MLA_NOPE), rot(k_r), zeros(LANES - MLA_NOPE - MLA_ROPE)], axis=1)
    dup = lambda t: jnp.concatenate(
        [t[:, kv * HEAD_DIM:(kv + 1) * HEAD_DIM] for kv in range(SWA_KV_HEADS) for _ in range(2)], axis=1)
    w_in2 = jnp.concatenate([c_q, c_kv, kr_blk, krr_blk, q_s * (HEAD_DIM ** -0.5 * LOG2E), dup(k_s), dup(v_s)],
                            axis=1)
    assert w_in2.shape[1] == _E_END
    r = w_uq.shape[0]
    uq = w_uq.reshape(r, MLA_HEADS, MLA_NOPE + MLA_ROPE)
    zq = jnp.zeros((r, MLA_HEADS, MLA_PAD - MLA_NOPE - MLA_ROPE), w_uq.dtype)
    uq_pad = jnp.concatenate([uq, zq], axis=-1).reshape(r, MLA_HEADS * MLA_PAD)
    uq_rot = jnp.concatenate([jnp.zeros_like(uq[..., :MLA_NOPE]), rot(uq[..., MLA_NOPE:]), zq],
                             axis=-1).reshape(r, MLA_HEADS * MLA_PAD)
    rk = w_ukv.shape[0]
    ukv = w_ukv.reshape(rk, MLA_HEADS, MLA_NOPE + MLA_V)
    uk_pad = jnp.concatenate([ukv[..., :MLA_NOPE],
                              jnp.zeros((rk, MLA_HEADS, MLA_PAD - MLA_NOPE), w_ukv.dtype)],
                             axis=-1).reshape(rk, MLA_HEADS * MLA_PAD)
    uv = ukv[..., MLA_NOPE:].reshape(rk, MLA_HEADS * MLA_V)
    return dict(w_in=w_in2.astype(BF16), q_norm=q_norm.reshape(1, -1), kv_norm=kv_norm.reshape(1, -1),
                w_uq=uq_pad.astype(BF16), w_uqr=uq_rot.astype(BF16), w_uk=uk_pad.astype(BF16),
                w_uv=uv.astype(BF16))


def _rope_tables(seq):
    inv = 1.0 / (ROPE_THETA ** (jnp.arange(0, MLA_ROPE, 2, dtype=F32) / MLA_ROPE))
    ang = jnp.arange(seq, dtype=F32)[:, None] * inv[None, :]
    cos = jnp.concatenate([jnp.cos(ang)] * 2, axis=1)
    sin = jnp.concatenate([jnp.sin(ang)] * 2, axis=1)
    ones = jnp.ones((seq, MLA_NOPE), F32)
    z_n = jnp.zeros((seq, MLA_NOPE), F32)
    z_t = jnp.zeros((seq, MLA_PAD - MLA_NOPE - MLA_ROPE), F32)
    kc = jnp.concatenate([z_n, cos, z_t], axis=1)
    ks = jnp.concatenate([z_n, sin, z_t], axis=1)
    scale = (MLA_NOPE + MLA_ROPE) ** -0.5 * LOG2E
    qc =jnp.tile(jnp.concatenate([ones, cos, z_t], axis=1) * scale, (1, MLA_HEADS))
    qs = jnp.tile(ks * scale, (1, MLA_HEADS))
    return dict(qc=qc, qs=qs, kc=kc, ks=ks)


def _t5_bucket(dist):
    exact = REL_BUCKETS // 2
    d = jnp.maximum(dist, 1).astype(F32)
    large = exact + (jnp.log(d / exact) / math.log(REL_MAX_DIST / exact)
                     * (REL_BUCKETS - exact)).astype(jnp.int32)
    large = jnp.minimum(large, REL_BUCKETS - 1)
    return jnp.where(dist < exact, dist, large)


def _swa_bias_kernel(rel_ref, out_ref):
    a = lax.broadcasted_iota(jnp.int32, (BLOCK_Q, 2 * BLOCK_Q), 0)
    col = lax.broadcasted_iota(jnp.int32, (BLOCK_Q, 2 * BLOCK_Q), 1)
    for tab, shift in enumerate((BLOCK_Q, 0)):
        dist = a + shift - col
        valid = jnp.logical_and(dist >= 0, dist < SWA_WINDOW)
        bucket = _t5_bucket(jnp.maximum(dist, 0))
        for hd in range(SWA_HEADS):
            bias = jnp.zeros((BLOCK_Q, 2 * BLOCK_Q), F32)
            for bk in range(REL_BUCKETS):
                bias = jnp.where(bucket == bk, rel_ref[bk, hd], bias)
            out_ref[tab, hd] = jnp.where(valid, bias * LOG2E, NEG_INF)


def _swa_bias(rel_bias):
    return pl.pallas_call(
        _swa_bias_kernel,
        in_specs=[pl.BlockSpec(memory_space=pltpu.SMEM)],
        out_shape=jax.ShapeDtypeStruct((2, SWA_HEADS, BLOCK_Q, 2 * BLOCK_Q), F32),
        name="swa_bias_table",
    )(rel_bias)


def kernel(x, p, rel_bias, ev_w_in, ev_q_norm, ev_w_uq, ev_kv_norm, ev_w_ukv, ev_sinks, ev_w_out,
           od_w_in, od_b_f, od_w_out, ln1_g, ln1_b, w_up, w_down, ln2_g, ln2_b,
           ple_w_proj, ple_w_gate, ple_b_gate):
    b, seq, d = x.shape
    m = b * seq
    tm = min(512, seq)
    tile = min(512, seq)
    assert d == D_MODEL and seq % tm == 0 and seq % tile == 0 and seq % BLOCK_Q == 0
    assert seq >= 2 * BLOCK_Q
    tabs = _rope_tables(seq)
    bias = _swa_bias(rel_bias)
    x2 = x.reshape(m, d)
    hd = FOX_HEADS * HEAD_DIM
    row2 = lambda v: v.reshape(1, -1)
    for i in range(DEPTH):
        j = i // 2
        if i % 2 == 0:
            w = _even_weights(ev_w_in[j], ev_q_norm[j], ev_w_uq[j], ev_kv_norm[j], ev_w_ukv[j])
            qm, km, vm, qsw, ksw, vsw = _even_proj(x2, w, tabs, seq, tm)
            r3 = lambda t: t.reshape(b, seq, t.shape[1])
            o_mla = _flash(r3(qm), r3(km), r3(vm), tile=tile)
            o_swa = _swa(r3(qsw), r3(ksw), r3(vsw), bias, ev_sinks[j])
            w_out = ev_w_out[j].astype(BF16)
            n_mla = MLA_HEADS * MLA_V
            parts = [o_mla.reshape(m, -1), o_swa.reshape(m, -1)]
            w_outs = [w_out[:n_mla], w_out[n_mla:]]
        else:
            wi = od_w_in[j]
            w = dict(wq=(wi[:, :hd] * (HEAD_DIM ** -0.5 * LOG2E)).astype(BF16),
                     wk=wi[:, hd:2 * hd].astype(BF16),
                     wv=wi[:, 2 * hd:3 * hd].astype(BF16),
                     wf=_pad_cols(wi[:, 3 * hd:], LANES).astype(BF16),
                     bf=_pad_cols(row2(od_b_f[j]), LANES))
            q, k, v, qx, kx = _odd_proj(x2, w, seq, tm)
            r3 = lambda t: t.reshape(b, seq, t.shape[1])
            o = _flash(r3(q), r3(k), r3(v), r3(qx), r3(kx), tile=tile)
            parts = [o.reshape(m, -1)]
            w_outs = [od_w_out[j].astype(BF16)]
        x2 = _layer_tail(parts, w_outs, x2, row2(ln1_g[i]), row2(ln1_b[i]),
                         w_up[i].astype(BF16), w_down[i].astype(BF16), row2(ln2_g[i]), row2(ln2_b[i]),
                         ple_w_gate[i].astype(BF16), row2(ple_b_gate[i]), p[i].reshape(m, D_PLE),
                         ple_w_proj[i].astype(BF16), tm, min(1024, D_FF))
    return x2.reshape(b, seq, d)
```

```python
import functools
import math

import jax
import jax.numpy as jnp
import numpy as np
from jax import lax
from jax.experimental import pallas as pl
from jax.experimental.pallas import tpu as pltpu

D_MODEL = 1024
HEAD_DIM = 64
MLA_HEADS = 8
MLA_NOPE = 64
MLA_ROPE = 32
MLA_V = 64
MLA_Q_LORA = 384
MLA_KV_LORA = 256
ROPE_THETA = 10000.0
SWA_HEADS = 8
SWA_KV_HEADS = 2
SWA_WINDOW = 128
REL_BUCKETS = 32
REL_MAX_DIST = 128
FOX_HEADS = 16
D_FF = 4 * D_MODEL
D_PLE = 256
BLOCK_Q = 128
DEPTH = 4
DN_ALPHA = (2 * DEPTH) ** 0.25
NORM_EPS = 1e-5
NEG_INF = -1e30

LANES = 128
MLA_PAD = 128
VMEM_LIMIT = 52 * 1024 * 1024
LOG2E = math.log2(math.e)
FLASH_PAIRS_PER_STEP = 4
ONES_ROWS = 16
DECAY_PARTS = 3
DECAY_LANES = 8

F32 = jnp.float32
BF16 = jnp.bfloat16

_E_CQ = 0
_E_CKV = _E_CQ + MLA_Q_LORA
_E_KR = _E_CKV + MLA_KV_LORA
_E_KRR = _E_KR + LANES
_E_QS = _E_KRR + LANES
_E_KS = _E_QS + SWA_HEADS * HEAD_DIM
_E_VS = _E_KS + 2 * SWA_KV_HEADS * HEAD_DIM
_E_END = _E_VS + 2 * SWA_KV_HEADS * HEAD_DIM


def _cparams(n_axes):
    return pltpu.CompilerParams(dimension_semantics=("arbitrary",) * n_axes,
                                vmem_limit_bytes=VMEM_LIMIT)


def _const_spec(shape):
    return pl.BlockSpec(shape, lambda *_: (0,) * len(shape))


def _layer_norm(y, g, b):
    mu = jnp.mean(y, axis=-1, keepdims=True)
    yc = y - mu
    var = jnp.mean(yc * yc, axis=-1, keepdims=True)
    return yc * lax.rsqrt(var + NORM_EPS) * g + b


def _rms_norm(y, g):
    return y * lax.rsqrt(jnp.mean(y * y, axis=-1, keepdims=True) + NORM_EPS) * g


def _even_proj_kernel(x_ref, w_in_ref, qn_ref, kvn_ref, w_uq_ref, w_uqr_ref, w_uk_ref, w_uv_ref,
                      qc_ref, qs_ref, kc_ref, ks_ref,
                      qm_ref, km_ref, vm_ref, qsw_ref, ksw_ref, vsw_ref):
    xb = x_ref[...].astype(BF16)
    h = jnp.dot(xb, w_in_ref[...], preferred_element_type=F32)
    cq = _rms_norm(h[:, _E_CQ:_E_CKV], qn_ref[...]).astype(BF16)
    ckv = _rms_norm(h[:, _E_CKV:_E_KR], kvn_ref[...]).astype(BF16)
    qa = jnp.dot(cq, w_uq_ref[...], preferred_element_type=F32)
    qb = jnp.dot(cq, w_uqr_ref[...], preferred_element_type=F32)
    qc = jnp.concatenate([qc_ref[...]] * MLA_HEADS, axis=1)
    qs = jnp.concatenate([qs_ref[...]] * MLA_HEADS, axis=1)
    qm_ref[...] = (qa * qc + qb * qs).astype(BF16)
    kr = h[:, _E_KR:_E_KRR] * kc_ref[...] + h[:, _E_KRR:_E_QS] * ks_ref[...]
    kn = jnp.dot(ckv, w_uk_ref[...], preferred_element_type=F32)
    for hd in range(MLA_HEADS):
        sl = slice(hd * MLA_PAD, (hd + 1) * MLA_PAD)
        km_ref[:, sl] = (kn[:, sl] + kr).astype(BF16)
    vm_ref[...] = lax.dot_general(w_uv_ref[...], ckv, (((1,), (1,)), ((), ())),
                                  preferred_element_type=F32).astype(BF16)
    qsw_ref[...] = h[:, _E_QS:_E_KS].astype(BF16)
    ksw_ref[...] = h[:, _E_KS:_E_VS].astype(BF16)
    vsw_ref[...] = h[:, _E_VS:_E_END].astype(BF16)


def _even_proj(x2, w, tabs, seq, tm):
    m = x2.shape[0]
    n_pos = seq // tm
    row = lambda n: pl.BlockSpec((tm, n), lambda i: (i, 0))
    pos = lambda n: pl.BlockSpec((tm, n), lambda i: (i % n_pos, 0))
    hq = MLA_HEADS * MLA_PAD
    out_shape = (
        jax.ShapeDtypeStruct((m, hq), BF16), jax.ShapeDtypeStruct((m, hq), BF16),
        jax.ShapeDtypeStruct((MLA_HEADS * MLA_V, m), BF16),
        jax.ShapeDtypeStruct((m, SWA_HEADS * HEAD_DIM), BF16),
        jax.ShapeDtypeStruct((m, 2 * SWA_KV_HEADS * HEAD_DIM), BF16),
        jax.ShapeDtypeStruct((m, 2 * SWA_KV_HEADS * HEAD_DIM), BF16),
    )
    return pl.pallas_call(
        _even_proj_kernel,
        grid=(m // tm,),
        in_specs=[row(D_MODEL), _const_spec(w["w_in"].shape), _const_spec((1, MLA_Q_LORA)),
                  _const_spec((1, MLA_KV_LORA)), _const_spec(w["w_uq"].shape),
                  _const_spec(w["w_uqr"].shape), _const_spec(w["w_uk"].shape),
                  _const_spec(w["w_uv"].shape), pos(LANES), pos(LANES), pos(LANES), pos(LANES)],
        out_specs=(row(hq), row(hq), pl.BlockSpec((MLA_HEADS * MLA_V, tm), lambda i: (0, i)),
                   row(SWA_HEADS * HEAD_DIM),
                   row(2 * SWA_KV_HEADS * HEAD_DIM), row(2 * SWA_KV_HEADS * HEAD_DIM)),
        out_shape=out_shape,
        compiler_params=_cparams(1),
        name="even_proj",
    )(x2, w["w_in"], w["q_norm"], w["kv_norm"], w["w_uq"], w["w_uqr"], w["w_uk"], w["w_uv"],
      tabs["qc"], tabs["qs"], tabs["kc"], tabs["ks"])


def _split3(c):
    hi = c.astype(BF16)
    r1 = c - hi.astype(F32)
    mid = r1.astype(BF16)
    lo = (r1 - mid.astype(F32)).astype(BF16)
    return hi, mid, lo


def _odd_proj_kernel(x_ref, wq_ref, wk_ref, wv_ref, wf_ref, bf_ref, place_ref, ones_ref,
                     q_ref, k_ref, v_ref, qx_ref, kx_ref, carry_ref, *, n_pos):
    tm = x_ref.shape[0]
    xb = x_ref[...].astype(BF16)
    q_ref[...] = jnp.dot(xb, wq_ref[...], preferred_element_type=F32).astype(BF16)
    k_ref[...] = jnp.dot(xb, wk_ref[...], preferred_element_type=F32).astype(BF16)
    v_ref[...] = lax.dot_general(wv_ref[...], xb, (((1,), (1,)), ((), ())),
                                 preferred_element_type=F32).astype(BF16)
    z = jnp.dot(xb, wf_ref[...], preferred_element_type=F32) + bf_ref[...]
    logf = jnp.minimum(z, 0.0) - jnp.log1p(jnp.exp(-jnp.abs(z)))

    @pl.when(pl.program_id(0) % n_pos == 0)
    def _():
        carry_ref[...] = jnp.zeros_like(carry_ref)

    rows = lax.broadcasted_iota(jnp.int32, (tm, LANES), 0)
    c = logf
    sh = 1
    while sh < tm:
        c = c + jnp.where(rows >= sh, pltpu.roll(c, sh, 0), 0.0)
        sh *= 2
    c = c + carry_ref[0:1, :]
    carry_ref[...] = jnp.broadcast_to(c[tm - 1:tm, :], carry_ref.shape)
    parts = jnp.concatenate(_split3(c * LOG2E), axis=1)
    placed = jnp.dot(parts, place_ref[...], preferred_element_type=F32) + ones_ref[...]
    qx_ref[...] = placed[:, :LANES].astype(BF16)
    kx_ref[...] = placed[:, LANES:].astype(BF16)


def _odd_proj(x2, w, seq, tm):
    m = x2.shape[0]
    n_pos = seq // tm
    row = lambda n: pl.BlockSpec((tm, n), lambda i: (i, 0))
    hd = FOX_HEADS * HEAD_DIM
    out_shape = ((jax.ShapeDtypeStruct((m, hd), BF16),) * 2 + (jax.ShapeDtypeStruct((hd, m), BF16),)
                 + (jax.ShapeDtypeStruct((m, LANES), BF16),) * 2)
    place, ones = _decay_placement()
    return pl.pallas_call(
        functools.partial(_odd_proj_kernel, n_pos=n_pos),
        grid=(m // tm,),
        in_specs=[row(D_MODEL), _const_spec((D_MODEL, hd)), _const_spec((D_MODEL, hd)),
                  _const_spec((D_MODEL, hd)), _const_spec((D_MODEL, LANES)), _const_spec((1, LANES)),
                  _const_spec(place.shape), _const_spec(ones.shape)],
        out_specs=(row(hd), row(hd), pl.BlockSpec((hd, tm), lambda i: (0, i)), row(LANES), row(LANES)),
        out_shape=out_shape,
        scratch_shapes=[pltpu.VMEM((8, LANES), F32)],
        compiler_params=_cparams(1),
        name="odd_proj",
    )(x2, w["wq"], w["wk"], w["wv"], w["wf"], w["bf"], place, ones)


def _decay_placement():
    place = np.zeros((DECAY_PARTS * LANES, 2 * LANES), np.float32)
    ones = np.zeros((1, 2 * LANES), np.float32)
    for h in range(FOX_HEADS):
        for t in range(DECAY_PARTS):
            place[t * LANES + h, DECAY_LANES * h + t] = 1.0
            place[t * LANES + h, LANES + DECAY_LANES * h + DECAY_PARTS + t] = -1.0
            ones[0, DECAY_LANES * h + DECAY_PARTS + t] = 1.0
            ones[0, LANES + DECAY_LANES * h + t] = 1.0
    return jnp.asarray(place, BF16), jnp.asarray(ones, F32)


def _flash_kernel(*refs, tile, seq, dqk, fox, pairs):
    if fox:
        q_ref, k_ref, vt_ref, qx_ref, kx_ref, o_ref, s_ref, mc_ref, m_ref, acc_ref, mask_ref = refs
    else:
        q_ref, k_ref, vt_ref, o_ref, s_ref, mc_ref, m_ref, acc_ref, mask_ref = refs
    nq = seq // tile
    n_steps = nq * (nq + 1) // 2
    first_pair = pl.program_id(1) * pairs
    lane = lax.broadcasted_iota(jnp.int32, (1, LANES), 1)
    lo = lane < HEAD_DIM
    nt = (((1,), (1,)), ((), ()))
    ones_rows = jnp.ones((ONES_ROWS, tile), BF16)

    def q_operands(qi):
        q0 = pl.multiple_of(qi * tile, tile)
        qa = []
        for pp in range(pairs):
            q_pair = q_ref[0, pl.ds(q0, tile), pp * dqk:(pp + 1) * dqk]
            if fox:
                zero = jnp.zeros_like(q_pair)
                qx_blk = qx_ref[0, pl.ds(q0, tile), :]
                for hh in range(2):
                    qm = jnp.where(lo, q_pair, zero) if hh == 0 else jnp.where(lo, zero, q_pair)
                    head = 2 * (first_pair + pp) + hh
                    qx = jnp.where(lane // DECAY_LANES == head, qx_blk, jnp.zeros_like(qx_blk))
                    qa.append(jnp.concatenate([qm, qx], axis=1))
            else:
                qa += [q_pair[:, :LANES], q_pair[:, LANES:]]
        return qa

    def stage_a(qi, j):
        qa = q_operands(qi)
        k0 = pl.multiple_of(j * tile, tile)
        mask = mask_ref[jnp.where(j == qi, 1, 0)]
        for pp in range(pairs):
            k_pair = k_ref[0, pl.ds(k0, tile), pp * dqk:(pp + 1) * dqk]
            if fox:
                k_pair = jnp.concatenate([k_pair, kx_ref[0, pl.ds(k0, tile), :]], axis=1)
            for hh in range(2):
                h = 2 * pp + hh
                kh = k_pair if fox else k_pair[:, hh * LANES:(hh + 1) * LANES]
                s = lax.dot_general(kh, qa[h], nt, preferred_element_type=F32)
                s = s + mask
                s_ref[h] = s
                mc_ref[h] = jnp.max(s, axis=0, keepdims=True)

    def stage_b(j):
        k0 = pl.multiple_of(j * tile, tile)
        m_cap = jnp.where(j == 0, NEG_INF, -NEG_INF)
        for pp in range(pairs):
            vt = jnp.concatenate([vt_ref[pp * LANES:(pp + 1) * LANES, pl.ds(k0, tile)], ones_rows],
                                 axis=0)
            for hh in range(2):
                h = 2 * pp + hh
                m_prev = jnp.minimum(m_ref[h], m_cap)
                m_new = jnp.maximum(m_prev, mc_ref[h])
                alpha = jnp.exp2(m_prev - m_new)
                p = jnp.exp2(s_ref[h] - m_new).astype(BF16)
                pv = jnp.dot(vt, p, preferred_element_type=F32)
                acc_ref[h] = acc_ref[h] * alpha + pv
                m_ref[h] = m_new

    def finalize(qi):
        q0 = pl.multiple_of(qi * tile, tile)
        for pp in range(pairs):
            for hh in range(2):
                a = acc_ref[2 * pp + hh]
                r0 = hh * HEAD_DIM
                out = a[r0:r0 + HEAD_DIM, :] / a[LANES:LANES + 1, :]
                o_ref[pp * LANES + r0:pp * LANES + r0 + HEAD_DIM, pl.ds(q0, tile)] = out.astype(o_ref.dtype)

    m_ref[...] = jnp.zeros(m_ref.shape, F32)
    acc_ref[...] = jnp.zeros(acc_ref.shape, F32)
    ahead = (lax.broadcasted_iota(jnp.int32, (tile, tile), 1)
             - lax.broadcasted_iota(jnp.int32, (tile, tile), 0))
    mask_ref[0] = jnp.zeros((tile, tile), F32)
    mask_ref[1] = jnp.where(ahead >= 0, 0.0, NEG_INF)
    stage_a(0, 0)

    def body(t, carry):
        qi, j = carry
        last = j == qi
        qn = jnp.where(last, qi + 1, qi)
        jn = jnp.where(last, 0, j + 1)
        stage_b(j)
        stage_a(qn, jn)

        @pl.when(last)
        def _():
            finalize(qi)

        return qn, jn

    lax.fori_loop(0, n_steps - 1, body, (jnp.int32(0), jnp.int32(0)))
    stage_b(nq - 1)
    finalize(nq - 1)


def _flash(q, k, vt, qx=None, kx=None, *, tile):
    b, seq, _ = q.shape
    all_pairs = vt.shape[0] // LANES
    dqk = q.shape[2] // all_pairs
    fox = qx is not None
    pairs = FLASH_PAIRS_PER_STEP
    heads = 2 * pairs
    assert all_pairs % pairs == 0
    blk = lambda n: pl.BlockSpec((1, seq, pairs * n), lambda bi, g: (bi, 0, g))
    blk_t = pl.BlockSpec((pairs * LANES, seq), lambda bi, g: (g, bi))
    in_specs = [blk(dqk), blk(dqk), blk_t]
    args = [q, k, vt]
    if fox:
        in_specs += [pl.BlockSpec((1, seq, LANES), lambda bi, g: (bi, 0, 0))] * 2
        args += [qx, kx]
    scratch = [pltpu.VMEM((heads, tile, tile), F32), pltpu.VMEM((heads, 1, tile), F32),
               pltpu.VMEM((heads, 1, tile), F32), pltpu.VMEM((heads, LANES + ONES_ROWS, tile), F32),
               pltpu.VMEM((2, tile, tile), F32)]
    return pl.pallas_call(
        functools.partial(_flash_kernel, tile=tile, seq=seq, dqk=dqk, fox=fox, pairs=pairs),
        grid=(b, all_pairs // pairs),
        in_specs=in_specs,
        out_specs=blk_t,
        out_shape=jax.ShapeDtypeStruct(vt.shape, BF16),
        scratch_shapes=scratch,
        compiler_params=_cparams(2),
        name="fox_attn" if fox else "mla_attn",
    )(*args)


def _swa_kernel(q_ref, k_ref, v_ref, bias_ref, sink_ref, o_ref, *, seq):
    nb = seq // BLOCK_Q
    lane = lax.broadcasted_iota(jnp.int32, (1, LANES), 1)
    lo = lane < HEAD_DIM
    group = SWA_HEADS // SWA_KV_HEADS
    band = 2 * BLOCK_Q
    ones_v = jnp.ones((band, LANES), BF16)
    nt = (((1,), (1,)), ((), ()))

    def block(n, carry):
        r0 = pl.multiple_of(n * BLOCK_Q, BLOCK_Q)
        b0 = pl.multiple_of(jnp.maximum(n - 1, 0) * BLOCK_Q, BLOCK_Q)
        tab = jnp.where(n == 0, 1, 0)
        for pair in range(SWA_HEADS // 2):
            kvh = (2 * pair) // group
            ksl = slice(kvh * LANES, (kvh + 1) * LANES)
            q_pair = q_ref[0, pl.ds(r0, BLOCK_Q), pair * LANES:(pair + 1) * LANES]
            k_band = k_ref[0, pl.ds(b0, band), ksl]
            va = jnp.concatenate([v_ref[0, pl.ds(b0, band), ksl], ones_v], axis=1)
            zq = jnp.zeros_like(q_pair)
            res = []
            for hh in range(2):
                hd = 2 * pair + hh
                qh = jnp.where(lo, q_pair, zq) if hh == 0 else jnp.where(lo, zq, q_pair)
                s = lax.dot_general(qh, k_band, nt, preferred_element_type=F32) + bias_ref[tab, hd]
                sink = sink_ref[hd] * LOG2E
                mx = jnp.maximum(jnp.max(s, axis=1, keepdims=True), sink)
                p = jnp.exp2(s - mx).astype(BF16)
                pv = jnp.dot(p, va, preferred_element_type=F32)
                den = pv[:, LANES:] + jnp.exp2(sink - mx)
                res.append(pv[:, :LANES] / den)
            o_ref[0, pl.ds(r0, BLOCK_Q), pair * LANES:(pair + 1) * LANES] = (
                jnp.where(lo, res[0], res[1]).astype(o_ref.dtype))
        return carry

    lax.fori_loop(0, nb, block, 0)


def _swa(q, k, v, bias, sinks):
    b, seq, dq = q.shape
    dk = k.shape[2]
    return pl.pallas_call(
        functools.partial(_swa_kernel, seq=seq),
        grid=(b,),
        in_specs=[pl.BlockSpec((1, seq, dq), lambda bi: (bi, 0, 0)),
                  pl.BlockSpec((1, seq, dk), lambda bi: (bi, 0, 0)),
                  pl.BlockSpec((1, seq, dk), lambda bi: (bi, 0, 0)),
                  _const_spec(bias.shape),
                  pl.BlockSpec(memory_space=pltpu.SMEM)],
        out_specs=pl.BlockSpec((1, seq, dq), lambda bi: (bi, 0, 0)),
        out_shape=jax.ShapeDtypeStruct((b, seq, dq), BF16),
        compiler_params=_cparams(1),
        name="swa_attn",
    )(q, k, v, bias, sinks)


def _tail_kernel(*refs, transposed, tf):
    n_parts = len(transposed)
    o_refs = refs[:n_parts]
    w_refs = refs[n_parts:2 * n_parts]
    (x_ref, g1_ref, b1_ref, wu_ref, wd_ref, g2_ref, b2_ref, wg_ref, bg_ref, p_ref, wp_ref,
     y_ref) = refs[2 * n_parts:]
    mix = None
    for o_ref, w_ref, is_t in zip(o_refs, w_refs, transposed):
        dims = (((0,), (0,)), ((), ())) if is_t else (((1,), (0,)), ((), ()))
        d = lax.dot_general(o_ref[...], w_ref[...], dims, preferred_element_type=F32)
        mix = d if mix is None else mix + d
    x1 = _layer_norm(DN_ALPHA * x_ref[...] + mix, g1_ref[...], b1_ref[...])
    xb = x1.astype(BF16)
    acc = None
    for f in range(D_FF // tf):
        hdn = jnp.maximum(jnp.dot(xb, wu_ref[:, f * tf:(f + 1) * tf], preferred_element_type=F32), 0.0)
        part = jnp.dot((hdn * hdn).astype(BF16), wd_ref[f * tf:(f + 1) * tf, :],
                       preferred_element_type=F32)
        acc = part if acc is None else acc + part
    y = _layer_norm(DN_ALPHA * x1 + acc, g2_ref[...], b2_ref[...])
    gate = jax.nn.sigmoid(jnp.dot(y.astype(BF16), wg_ref[...], preferred_element_type=F32) + bg_ref[...])
    emb = jnp.dot(p_ref[...].astype(BF16), wp_ref[...], preferred_element_type=F32)
    y_ref[...] = y + gate * emb


def _resident_spec(shape):
    return pl.BlockSpec(shape, lambda *_: (0,) * len(shape), pipeline_mode=pl.Buffered(1))


def _layer_tail(parts, transposed, w_outs, x2, g1, b1, w_up, w_down, g2, b2, w_gate, b_gate, p2, w_proj,
                tm, tf):
    m = x2.shape[0]
    row = lambda c: pl.BlockSpec((tm, c), lambda i: (i, 0))
    col = lambda c: pl.BlockSpec((c, tm), lambda i: (0, i))
    vec = _const_spec((1, D_MODEL))
    return pl.pallas_call(
        functools.partial(_tail_kernel, transposed=tuple(transposed), tf=tf),
        grid=(m // tm,),
        in_specs=[col(o.shape[0]) if t else row(o.shape[1]) for o, t in zip(parts, transposed)]
        + [_resident_spec(w.shape) for w in w_outs]
        + [row(D_MODEL), vec, vec, _resident_spec((D_MODEL, D_FF)), _resident_spec((D_FF, D_MODEL)),
           vec, vec, _resident_spec((D_MODEL, D_MODEL)), vec, row(D_PLE),
           _resident_spec((D_PLE, D_MODEL))],
        out_specs=row(D_MODEL),
        out_shape=jax.ShapeDtypeStruct((m, D_MODEL), F32),
        compiler_params=_cparams(1),
        name="layer_tail",
    )(*parts, *w_outs, x2, g1, b1, w_up, w_down, g2, b2, w_gate, b_gate, p2, w_proj)


def _pad_cols(w, n):
    return jnp.pad(w, ((0, 0), (0, n - w.shape[1])))


def _even_weights(w_in, q_norm, w_uq, kv_norm, w_ukv):
    sizes = [MLA_Q_LORA, MLA_KV_LORA, MLA_ROPE, SWA_HEADS * HEAD_DIM, SWA_KV_HEADS * HEAD_DIM]
    c_q, c_kv, k_r, q_s, k_s, v_s = jnp.split(w_in, np.cumsum(sizes).tolist(), axis=1)
    half = MLA_ROPE // 2
    rot = lambda wr: jnp.concatenate([-wr[..., half:], wr[..., :half]], axis=-1)
    d = w_in.shape[0]
    zeros = lambda n: jnp.zeros((d, n), w_in.dtype)
    kr_blk = jnp.concatenate([zeros(MLA_NOPE), k_r, zeros(LANES - MLA_NOPE - MLA_ROPE)], axis=1)
    krr_blk = jnp.concatenate([zeros(MLA_NOPE), rot(k_r), zeros(LANES - MLA_NOPE - MLA_ROPE)], axis=1)
    dup = lambda t: jnp.concatenate(
        [t[:, kv * HEAD_DIM:(kv + 1) * HEAD_DIM] for kv in range(SWA_KV_HEADS) for _ in range(2)], axis=1)
    w_in2 = jnp.concatenate([c_q, c_kv, kr_blk, krr_blk, q_s * (HEAD_DIM ** -0.5 * LOG2E), dup(k_s), dup(v_s)],
                            axis=1)
    assert w_in2.shape[1] == _E_END
    r = w_uq.shape[0]
    uq = w_uq.reshape(r, MLA_HEADS, MLA_NOPE + MLA_ROPE)
    zq = jnp.zeros((r, MLA_HEADS, MLA_PAD - MLA_NOPE - MLA_ROPE), w_uq.dtype)
    uq_pad = jnp.concatenate([uq, zq], axis=-1).reshape(r, MLA_HEADS * MLA_PAD)
    uq_rot = jnp.concatenate([jnp.zeros_like(uq[..., :MLA_NOPE]), rot(uq[..., MLA_NOPE:]), zq],
                             axis=-1).reshape(r, MLA_HEADS * MLA_PAD)
    rk = w_ukv.shape[0]
    ukv = w_ukv.reshape(rk, MLA_HEADS, MLA_NOPE + MLA_V)
    uk_pad = jnp.concatenate([ukv[..., :MLA_NOPE],
                              jnp.zeros((rk, MLA_HEADS, MLA_PAD - MLA_NOPE), w_ukv.dtype)],
                             axis=-1).reshape(rk, MLA_HEADS * MLA_PAD)
    uv = ukv[..., MLA_NOPE:].reshape(rk, MLA_HEADS * MLA_V)
    return dict(w_in=w_in2.astype(BF16), q_norm=q_norm.reshape(1, -1), kv_norm=kv_norm.reshape(1, -1),
                w_uq=uq_pad.astype(BF16), w_uqr=uq_rot.astype(BF16), w_uk=uk_pad.astype(BF16),
                w_uv=uv.T.astype(BF16))


def _rope_tables(seq):
    inv = 1.0 / (ROPE_THETA ** (jnp.arange(0, MLA_ROPE, 2, dtype=F32) / MLA_ROPE))
    ang = jnp.arange(seq, dtype=F32)[:, None] * inv[None, :]
    cos = jnp.concatenate([jnp.cos(ang)] * 2, axis=1)
    sin = jnp.concatenate([jnp.sin(ang)] * 2, axis=1)
    ones = jnp.ones((seq, MLA_NOPE), F32)
    z_n = jnp.zeros((seq, MLA_NOPE), F32)
    z_t = jnp.zeros((seq, MLA_PAD - MLA_NOPE - MLA_ROPE), F32)
    kc = jnp.concatenate([z_n, cos, z_t], axis=1)
    ks = jnp.concatenate([z_n, sin, z_t], axis=1)
    scale = (MLA_NOPE + MLA_ROPE) ** -0.5 * LOG2E
    qc = jnp.concatenate([ones, cos, z_t], axis=1) * scale
    qs = ks * scale
    return dict(qc=qc, qs=qs, kc=kc, ks=ks)


def _t5_bucket(dist):
    exact = REL_BUCKETS // 2
    d = jnp.maximum(dist, 1).astype(F32)
    large = exact + (jnp.log(d / exact) / math.log(REL_MAX_DIST / exact)
                     * (REL_BUCKETS - exact)).astype(jnp.int32)
    large = jnp.minimum(large, REL_BUCKETS - 1)
    return jnp.where(dist < exact, dist, large)


def _swa_bias_kernel(rel_ref, out_ref):
    a = lax.broadcasted_iota(jnp.int32, (BLOCK_Q, 2 * BLOCK_Q), 0)
    col = lax.broadcasted_iota(jnp.int32, (BLOCK_Q, 2 * BLOCK_Q), 1)
    for tab, shift in enumerate((BLOCK_Q, 0)):
        dist = a + shift - col
        valid = jnp.logical_and(dist >= 0, dist < SWA_WINDOW)
        bucket = _t5_bucket(jnp.maximum(dist, 0))
        for hd in range(SWA_HEADS):
            bias = jnp.zeros((BLOCK_Q, 2 * BLOCK_Q), F32)
            for bk in range(REL_BUCKETS):
                bias = jnp.where(bucket == bk, rel_ref[bk, hd], bias)
            out_ref[tab, hd] = jnp.where(valid, bias * LOG2E, NEG_INF)


def _swa_bias(rel_bias):
    return pl.pallas_call(
        _swa_bias_kernel,
        in_specs=[pl.BlockSpec(memory_space=pltpu.SMEM)],
        out_shape=jax.ShapeDtypeStruct((2, SWA_HEADS, BLOCK_Q, 2 * BLOCK_Q), F32),
        name="swa_bias_table",
    )(rel_bias)


def kernel(x, p, rel_bias, ev_w_in, ev_q_norm, ev_w_uq, ev_kv_norm, ev_w_ukv, ev_sinks, ev_w_out,
           od_w_in, od_b_f, od_w_out, ln1_g, ln1_b, w_up, w_down, ln2_g, ln2_b,
           ple_w_proj, ple_w_gate, ple_b_gate):
    b, seq, d = x.shape
    m = b * seq
    tm = min(512, seq)
    tile = min(512, seq)
    assert d == D_MODEL and seq % tm == 0 and seq % tile == 0 and seq % BLOCK_Q == 0
    assert seq >= 2 * BLOCK_Q
    tabs = _rope_tables(seq)
    bias = _swa_bias(rel_bias)
    x2 = x.reshape(m, d)
    hd = FOX_HEADS * HEAD_DIM
    row2 = lambda v: v.reshape(1, -1)
    for i in range(DEPTH):
        j = i // 2
        if i % 2 == 0:
            w = _even_weights(ev_w_in[j], ev_q_norm[j], ev_w_uq[j], ev_kv_norm[j], ev_w_ukv[j])
            qm, km, vm, qsw, ksw, vsw = _even_proj(x2, w, tabs, seq, tm)
            r3 = lambda t: t.reshape(b, seq, t.shape[1])
            o_mla_t = _flash(r3(qm), r3(km), vm, tile=tile)
            o_swa = _swa(r3(qsw), r3(ksw), r3(vsw), bias, ev_sinks[j])
            w_out = ev_w_out[j].astype(BF16)
            n_mla = MLA_HEADS * MLA_V
            parts = [o_mla_t, o_swa.reshape(m, -1)]
            transposed = [True, False]
            w_outs = [w_out[:n_mla], w_out[n_mla:]]
        else:
            wi = od_w_in[j]
            w = dict(wq=(wi[:, :hd] * (HEAD_DIM ** -0.5 * LOG2E)).astype(BF16),
                     wk=wi[:, hd:2 * hd].astype(BF16),
                     wv=wi[:, 2 * hd:3 * hd].T.astype(BF16),
                     wf=_pad_cols(wi[:, 3 * hd:], LANES).astype(BF16),
                     bf=_pad_cols(row2(od_b_f[j]), LANES))
            q, k, v, qx, kx = _odd_proj(x2, w, seq, tm)
            r3 = lambda t: t.reshape(b, seq, t.shape[1])
            parts = [_flash(r3(q), r3(k), v, r3(qx), r3(kx), tile=tile)]
            transposed = [True]
            w_outs = [od_w_out[j].astype(BF16)]
        x2 = _layer_tail(parts, transposed, w_outs, x2, row2(ln1_g[i]), row2(ln1_b[i]),
                         w_up[i].astype(BF16), w_down[i].astype(BF16), row2(ln2_g[i]), row2(ln2_b[i]),
                         ple_w_gate[i].astype(BF16), row2(ple_b_gate[i]), p[i].reshape(m, D_PLE),
                         ple_w_proj[i].astype(BF16), tm, min(1024, D_FF))
    return x2.reshape(b, seq, d)
```

```python
import functools
import math

import jax
import jax.numpy as jnp
import numpy as np
from jax import lax
from jax.experimental import pallas as pl
from jax.experimental.pallas import tpu as pltpu

D_MODEL = 1024
HEAD_DIM = 64
MLA_HEADS = 8
MLA_NOPE = 64
MLA_ROPE = 32
MLA_V = 64
MLA_Q_LORA = 384
MLA_KV_LORA = 256
ROPE_THETA = 10000.0
SWA_HEADS = 8
SWA_KV_HEADS = 2
SWA_WINDOW = 128
REL_BUCKETS = 32
REL_MAX_DIST = 128
FOX_HEADS = 16
D_FF = 4 * D_MODEL
D_PLE = 256
BLOCK_Q = 128
DEPTH = 4
DN_ALPHA = (2 * DEPTH) ** 0.25
NORM_EPS = 1e-5
NEG_INF = -1e30

LANES = 128
MLA_PAD = 128
VMEM_LIMIT = 52 * 1024 * 1024
LOG2E = math.log2(math.e)
FLASH_PAIRS_PER_STEP = 4
SWA_BLOCKS_PER_STEP = 8
DECAY_PARTS = 3
DECAY_LANES = 8

F32 = jnp.float32
BF16 = jnp.bfloat16

_E_CQ = 0
_E_CKV = _E_CQ + MLA_Q_LORA
_E_KR = _E_CKV + MLA_KV_LORA
_E_KRR = _E_KR + LANES
_E_QS = _E_KRR + LANES
_E_KS = _E_QS + SWA_HEADS * HEAD_DIM
_E_VS = _E_KS + 2 * SWA_KV_HEADS * HEAD_DIM
_E_END = _E_VS + 2 * SWA_KV_HEADS * HEAD_DIM


def _cparams(n_axes):
    return pltpu.CompilerParams(dimension_semantics=("arbitrary",) * n_axes,
                                vmem_limit_bytes=VMEM_LIMIT)


def _const_spec(shape):
    return pl.BlockSpec(shape, lambda *_: (0,) * len(shape), pipeline_mode=pl.Buffered(1))


def _layer_norm(y, g, b):
    mu = jnp.mean(y, axis=-1, keepdims=True)
    yc = y - mu
    var = jnp.mean(yc * yc, axis=-1, keepdims=True)
    return yc * lax.rsqrt(var + NORM_EPS) * g + b


def _rms_norm(y, g):
    return y * lax.rsqrt(jnp.mean(y * y, axis=-1, keepdims=True) + NORM_EPS) * g


def _even_proj_kernel(x_ref, w_in_ref, qn_ref, kvn_ref, w_uq_ref, w_uqr_ref, w_uk_ref, w_uv_ref,
                      qc_ref, qs_ref, kc_ref, ks_ref,
                      qm_ref, km_ref, vm_ref, qsw_ref, ksw_ref, vsw_ref):
    xb = x_ref[...].astype(BF16)
    h = jnp.dot(xb, w_in_ref[...], preferred_element_type=F32)
    cq = _rms_norm(h[:, _E_CQ:_E_CKV], qn_ref[...]).astype(BF16)
    ckv = _rms_norm(h[:, _E_CKV:_E_KR], kvn_ref[...]).astype(BF16)
    qa = jnp.dot(cq, w_uq_ref[...], preferred_element_type=F32)
    qb = jnp.dot(cq, w_uqr_ref[...], preferred_element_type=F32)
    qc = jnp.concatenate([qc_ref[...]] * MLA_HEADS, axis=1)
    qs = jnp.concatenate([qs_ref[...]] * MLA_HEADS, axis=1)
    qm_ref[...] = (qa * qc + qb * qs).astype(BF16)
    kr = h[:, _E_KR:_E_KRR] * kc_ref[...] + h[:, _E_KRR:_E_QS] * ks_ref[...]
    kn = jnp.dot(ckv, w_uk_ref[...], preferred_element_type=F32)
    for hd in range(MLA_HEADS):
        sl = slice(hd * MLA_PAD, (hd + 1) * MLA_PAD)
        km_ref[:, sl] = (kn[:, sl] + kr).astype(BF16)
    vm_ref[...] = jnp.dot(ckv, w_uv_ref[...], preferred_element_type=F32).astype(BF16)
    qsw_ref[...] = h[:, _E_QS:_E_KS].astype(BF16)
    ksw_ref[...] = h[:, _E_KS:_E_VS].astype(BF16)
    vsw_ref[...] = h[:, _E_VS:_E_END].astype(BF16)


def _even_proj(x2, w, tabs, seq, tm):
    m = x2.shape[0]
    n_pos = seq // tm
    row = lambda n: pl.BlockSpec((tm, n), lambda i: (i, 0))
    pos = lambda n: pl.BlockSpec((tm, n), lambda i: (i % n_pos, 0))
    hq = MLA_HEADS * MLA_PAD
    out_shape = (
        jax.ShapeDtypeStruct((m, hq), BF16), jax.ShapeDtypeStruct((m, hq), BF16),
        jax.ShapeDtypeStruct((m, MLA_HEADS * MLA_V), BF16),
        jax.ShapeDtypeStruct((m, SWA_HEADS * HEAD_DIM), BF16),
        jax.ShapeDtypeStruct((m, 2 * SWA_KV_HEADS * HEAD_DIM), BF16),
        jax.ShapeDtypeStruct((m, 2 * SWA_KV_HEADS * HEAD_DIM), BF16),
    )
    return pl.pallas_call(
        _even_proj_kernel,
        grid=(m // tm,),
        in_specs=[row(D_MODEL), _const_spec(w["w_in"].shape), _const_spec((1, MLA_Q_LORA)),
                  _const_spec((1, MLA_KV_LORA)), _const_spec(w["w_uq"].shape),
                  _const_spec(w["w_uqr"].shape), _const_spec(w["w_uk"].shape),
                  _const_spec(w["w_uv"].shape), pos(LANES), pos(LANES), pos(LANES), pos(LANES)],
        out_specs=(row(hq), row(hq), row(MLA_HEADS * MLA_V), row(SWA_HEADS * HEAD_DIM),
                   row(2 * SWA_KV_HEADS * HEAD_DIM), row(2 * SWA_KV_HEADS * HEAD_DIM)),
        out_shape=out_shape,
        compiler_params=_cparams(1),
        name="even_proj",
    )(x2, w["w_in"], w["q_norm"], w["kv_norm"], w["w_uq"], w["w_uqr"], w["w_uk"], w["w_uv"],
      tabs["qc"], tabs["qs"], tabs["kc"], tabs["ks"])


def _split3(c):
    hi = c.astype(BF16)
    r1 = c - hi.astype(F32)
    mid = r1.astype(BF16)
    lo = (r1 - mid.astype(F32)).astype(BF16)
    return hi, mid, lo


def _odd_proj_kernel(x_ref, wq_ref, wk_ref, wv_ref, wf_ref, bf_ref, place_ref, ones_ref,
                     q_ref, k_ref, v_ref, qx_ref, kx_ref, carry_ref, *, n_pos):
    tm = x_ref.shape[0]
    xb = x_ref[...].astype(BF16)
    q_ref[...] = jnp.dot(xb, wq_ref[...], preferred_element_type=F32).astype(BF16)
    k_ref[...] = jnp.dot(xb, wk_ref[...], preferred_element_type=F32).astype(BF16)
    v_ref[...] = jnp.dot(xb, wv_ref[...], preferred_element_type=F32).astype(BF16)
    z = jnp.dot(xb, wf_ref[...], preferred_element_type=F32) + bf_ref[...]
    logf = jnp.minimum(z, 0.0) - jnp.log1p(jnp.exp(-jnp.abs(z)))

    @pl.when(pl.program_id(0) % n_pos == 0)
    def _():
        carry_ref[...] = jnp.zeros_like(carry_ref)

    rows = lax.broadcasted_iota(jnp.int32, (tm, LANES), 0)
    c = logf
    sh = 1
    while sh < tm:
        c = c + jnp.where(rows >= sh, pltpu.roll(c, sh, 0), 0.0)
        sh *= 2
    c = c + carry_ref[0:1, :]
    carry_ref[...] = jnp.broadcast_to(c[tm - 1:tm, :], carry_ref.shape)
    parts = jnp.concatenate(_split3(c * LOG2E), axis=1)
    placed = jnp.dot(parts, place_ref[...], preferred_element_type=F32) + ones_ref[...]
    qx_ref[...] = placed[:, :LANES].astype(BF16)
    kx_ref[...] = placed[:, LANES:].astype(BF16)


def _odd_proj(x2, w, seq, tm):
    m = x2.shape[0]
    n_pos = seq // tm
    row = lambda n: pl.BlockSpec((tm, n), lambda i: (i, 0))
    hd = FOX_HEADS * HEAD_DIM
    out_shape = (jax.ShapeDtypeStruct((m, hd), BF16),) * 3 + (jax.ShapeDtypeStruct((m, LANES), BF16),) * 2
    place, ones = _decay_placement()
    return pl.pallas_call(
        functools.partial(_odd_proj_kernel, n_pos=n_pos),
        grid=(m // tm,),
        in_specs=[row(D_MODEL), _const_spec((D_MODEL, hd)), _const_spec((D_MODEL, hd)),
                  _const_spec((D_MODEL, hd)), _const_spec((D_MODEL, LANES)), _const_spec((1, LANES)),
                  _const_spec(place.shape), _const_spec(ones.shape)],
        out_specs=(row(hd), row(hd), row(hd), row(LANES), row(LANES)),
        out_shape=out_shape,
        scratch_shapes=[pltpu.VMEM((8, LANES), F32)],
        compiler_params=_cparams(1),
        name="odd_proj",
    )(x2, w["wq"], w["wk"], w["wv"], w["wf"], w["bf"], place, ones)


def _decay_placement():
    place = np.zeros((DECAY_PARTS * LANES, 2 * LANES), np.float32)
    ones = np.zeros((1, 2 * LANES), np.float32)
    for h in range(FOX_HEADS):
        for t in range(DECAY_PARTS):
            place[t * LANES + h, DECAY_LANES * h + t] = 1.0
            place[t * LANES + h, LANES + DECAY_LANES * h + DECAY_PARTS + t] = -1.0
            ones[0, DECAY_LANES * h + DECAY_PARTS + t] = 1.0
            ones[0, LANES + DECAY_LANES * h + t] = 1.0
    return jnp.asarray(place, BF16), jnp.asarray(ones, F32)


def _flash_kernel(*refs, tile, seq, dqk, fox, pairs):
    if fox:
        q_ref, k_ref, v_ref, qx_ref, kx_ref, o_ref, s_ref, mc_ref, m_ref, acc_ref, mask_ref = refs
    else:
        q_ref, k_ref, v_ref, o_ref, s_ref, mc_ref, m_ref, acc_ref, mask_ref = refs
    nq = seq // tile
    n_steps = nq * (nq + 1) // 2
    first_pair = pl.program_id(1) * pairs
    lane = lax.broadcasted_iota(jnp.int32, (1, LANES), 1)
    lo = lane < HEAD_DIM
    reps = tile // LANES
    nt = (((1,), (1,)), ((), ()))
    lane_full = lax.broadcasted_iota(jnp.int32, (tile, LANES), 1)
    ones_lo = jnp.where(lane_full < HEAD_DIM, 1.0, 0.0).astype(BF16)
    ones_hi = jnp.where(lane_full < HEAD_DIM, 0.0, 1.0).astype(BF16)

    def q_operands(qi):
        q0 = pl.multiple_of(qi * tile, tile)
        qa = []
        for pp in range(pairs):
            q_pair = q_ref[0, pl.ds(q0, tile), pp * dqk:(pp + 1) * dqk]
            if fox:
                zero = jnp.zeros_like(q_pair)
                qx_blk = qx_ref[0, pl.ds(q0, tile), :]
                for hh in range(2):
                    qm = jnp.where(lo, q_pair, zero) if hh == 0 else jnp.where(lo, zero, q_pair)
                    head = 2 * (first_pair + pp) + hh
                    qx = jnp.where(lane // DECAY_LANES == head, qx_blk, jnp.zeros_like(qx_blk))
                    qa.append(jnp.concatenate([qm, qx], axis=1))
            else:
                qa += [q_pair[:, :LANES], q_pair[:, LANES:]]
        return qa

    def stage_a(qi, j):
        qa = q_operands(qi)
        k0 = pl.multiple_of(j * tile, tile)
        mask = mask_ref[jnp.where(j == qi, 1, 0)]
        for pp in range(pairs):
            k_pair = k_ref[0, pl.ds(k0, tile), pp * dqk:(pp + 1) * dqk]
            if fox:
                k_pair = jnp.concatenate([k_pair, kx_ref[0, pl.ds(k0, tile), :]], axis=1)
            for hh in range(2):
                h = 2 * pp + hh
                kh = k_pair if fox else k_pair[:, hh * LANES:(hh + 1) * LANES]
                s = lax.dot_general(qa[h], kh, nt, preferred_element_type=F32)
                s = s + mask
                s_ref[h] = s
                mc_ref[h] = jnp.broadcast_to(jnp.max(s, axis=1, keepdims=True), (tile, LANES))

    def stage_b(j):
        k0 = pl.multiple_of(j * tile, tile)
        m_cap = jnp.where(j == 0, NEG_INF, -NEG_INF)
        for pp in range(pairs):
            v_pair = v_ref[0, pl.ds(k0, tile), pp * LANES:(pp + 1) * LANES]
            zero = jnp.zeros_like(v_pair)
            v_stack = jnp.concatenate(
                [jnp.concatenate([jnp.where(lo, v_pair, zero), ones_lo], axis=1),
                 jnp.concatenate([jnp.where(lo, zero, v_pair), ones_hi], axis=1)],
                axis=0)
            ps = []
            alphas = []
            for hh in range(2):
                h = 2 * pp + hh
                m_prev = jnp.minimum(m_ref[h], m_cap)
                m_new = jnp.maximum(m_prev, mc_ref[h])
                alphas.append(jnp.exp2(m_prev - m_new))
                ps.append(jnp.exp2(s_ref[h] - jnp.concatenate([m_new] * reps, axis=1)).astype(BF16))
                m_ref[h] = m_new
            pv = jnp.dot(jnp.concatenate(ps, axis=1), v_stack, preferred_element_type=F32)
            alpha = jnp.where(lo, alphas[0], alphas[1])
            acc_ref[pp] = acc_ref[pp] * jnp.concatenate([alpha, alpha], axis=1) + pv

    def finalize(qi):
        q0 = pl.multiple_of(qi * tile, tile)
        for pp in range(pairs):
            a = acc_ref[pp]
            o_ref[0, pl.ds(q0, tile), pp * LANES:(pp + 1) * LANES] = (
                a[:, :LANES] / a[:, LANES:]).astype(o_ref.dtype)

    m_ref[...] = jnp.zeros(m_ref.shape, F32)
    acc_ref[...] = jnp.zeros(acc_ref.shape, F32)
    dcol = (lax.broadcasted_iota(jnp.int32, (tile, tile), 1)
            - lax.broadcasted_iota(jnp.int32, (tile, tile), 0))
    mask_ref[0] = jnp.zeros((tile, tile), F32)
    mask_ref[1] = jnp.where(dcol <= 0, 0.0, NEG_INF)
    stage_a(0, 0)

    def body(t, carry):
        qi, j = carry
        last = j == qi
        qn = jnp.where(last, qi + 1, qi)
        jn = jnp.where(last, 0, j + 1)
        stage_b(j)
        stage_a(qn, jn)

        @pl.when(last)
        def _():
            finalize(qi)

        return qn, jn

    lax.fori_loop(0, n_steps - 1, body, (jnp.int32(0), jnp.int32(0)))
    stage_b(nq - 1)
    finalize(nq - 1)


def _flash(q, k, v, qx=None, kx=None, *, tile):
    b, seq, _ = q.shape
    all_pairs = v.shape[2] // LANES
    dqk = q.shape[2] // all_pairs
    fox = qx is not None
    pairs = min(FLASH_PAIRS_PER_STEP, all_pairs)
    heads = 2 * pairs
    assert all_pairs % pairs == 0
    blk = lambda n: pl.BlockSpec((1, seq, pairs * n), lambda bi, g: (bi, 0, g))
    in_specs = [blk(dqk), blk(dqk), blk(LANES)]
    args = [q, k, v]
    if fox:
        in_specs += [pl.BlockSpec((1, seq, LANES), lambda bi, g: (bi, 0, 0))] * 2
        args += [qx, kx]
    scratch = [pltpu.VMEM((heads, tile, tile), F32), pltpu.VMEM((heads, tile, LANES), F32),
               pltpu.VMEM((heads, tile, LANES), F32), pltpu.VMEM((pairs, tile, 2 * LANES), F32),
               pltpu.VMEM((2, tile, tile), F32)]
    return pl.pallas_call(
        functools.partial(_flash_kernel, tile=tile, seq=seq, dqk=dqk, fox=fox, pairs=pairs),
        grid=(b, all_pairs // pairs),
        in_specs=in_specs,
        out_specs=blk(LANES),
        out_shape=jax.ShapeDtypeStruct((b, seq, all_pairs * LANES), BF16),
        scratch_shapes=scratch,
        compiler_params=_cparams(2),
        name="fox_attn" if fox else "mla_attn",
    )(*args)


def _swa_kernel(q_ref, k_ref, v_ref, bias_ref, sink_ref, o_ref, *, seq):
    nb = seq // BLOCK_Q
    lane = lax.broadcasted_iota(jnp.int32, (1, LANES), 1)
    lo = lane < HEAD_DIM
    group = SWA_HEADS // SWA_KV_HEADS
    band = 2 * BLOCK_Q
    ones_v = jnp.ones((band, LANES), BF16)
    nt = (((1,), (1,)), ((), ()))

    def block(n):
        r0 = pl.multiple_of(n * BLOCK_Q, BLOCK_Q)
        b0 = pl.multiple_of(jnp.maximum(n - 1, 0) * BLOCK_Q, BLOCK_Q)
        tab = jnp.where(n == 0, 1, 0)
        for pair in range(SWA_HEADS // 2):
            kvh = (2 * pair) // group
            ksl = slice(kvh * LANES, (kvh + 1) * LANES)
            q_pair = q_ref[0, pl.ds(r0, BLOCK_Q), pair * LANES:(pair + 1) * LANES]
            k_band = k_ref[0, pl.ds(b0, band), ksl]
            va = jnp.concatenate([v_ref[0, pl.ds(b0, band), ksl], ones_v], axis=1)
            zq = jnp.zeros_like(q_pair)
            res = []
            for hh in range(2):
                hd = 2 * pair + hh
                qh = jnp.where(lo, q_pair, zq) if hh == 0 else jnp.where(lo, zq, q_pair)
                s = lax.dot_general(qh, k_band, nt, preferred_element_type=F32) + bias_ref[tab, hd]
                sink = sink_ref[hd] * LOG2E
                mx = jnp.maximum(jnp.max(s, axis=1, keepdims=True), sink)
                p = jnp.exp2(s - mx).astype(BF16)
                pv = jnp.dot(p, va, preferred_element_type=F32)
                den = pv[:, LANES:] + jnp.exp2(sink - mx)
                res.append(pv[:, :LANES] / den)
            o_ref[0, pl.ds(r0, BLOCK_Q), pair * LANES:(pair + 1) * LANES] = (
                jnp.where(lo, res[0], res[1]).astype(o_ref.dtype))

    unroll = SWA_BLOCKS_PER_STEP if nb % SWA_BLOCKS_PER_STEP == 0 else 1

    def body(i, carry):
        for u in range(unroll):
            block(i * unroll + u)
        return carry

    lax.fori_loop(0, nb // unroll, body, 0)


def _swa(q, k, v, bias, sinks):
    b, seq, dq = q.shape
    dk = k.shape[2]
    return pl.pallas_call(
        functools.partial(_swa_kernel, seq=seq),
        grid=(b,),
        in_specs=[pl.BlockSpec((1, seq, dq), lambda bi: (bi, 0, 0)),
                  pl.BlockSpec((1, seq, dk), lambda bi: (bi, 0, 0)),
                  pl.BlockSpec((1, seq, dk), lambda bi: (bi, 0, 0)),
                  _const_spec(bias.shape),
                  pl.BlockSpec(memory_space=pltpu.SMEM)],
        out_specs=pl.BlockSpec((1, seq, dq), lambda bi: (bi, 0, 0)),
        out_shape=jax.ShapeDtypeStruct((b, seq, dq), BF16),
        compiler_params=_cparams(1),
        name="swa_attn",
    )(q, k, v, bias, sinks)


def _tail_kernel(*refs, n_parts, tf):
    o_refs = refs[:n_parts]
    w_refs = refs[n_parts:2 * n_parts]
    (x_ref, g1_ref, b1_ref, wu_ref, wd_ref, g2_ref, b2_ref, wg_ref, bg_ref, p_ref, wp_ref,
     y_ref) = refs[2 * n_parts:]
    mix = None
    for o_ref, w_ref in zip(o_refs, w_refs):
        d = jnp.dot(o_ref[...], w_ref[...], preferred_element_type=F32)
        mix = d if mix is None else mix + d
    x1 = _layer_norm(DN_ALPHA * x_ref[...] + mix, g1_ref[...], b1_ref[...])
    xb = x1.astype(BF16)
    acc = None
    for f in range(D_FF // tf):
        hdn = jnp.maximum(jnp.dot(xb, wu_ref[:, f * tf:(f + 1) * tf], preferred_element_type=F32), 0.0)
        part = jnp.dot((hdn * hdn).astype(BF16), wd_ref[f * tf:(f + 1) * tf, :],
                       preferred_element_type=F32)
        acc = part if acc is None else acc + part
    y = _layer_norm(DN_ALPHA * x1 + acc, g2_ref[...], b2_ref[...])
    gate = jax.nn.sigmoid(jnp.dot(y.astype(BF16), wg_ref[...], preferred_element_type=F32) + bg_ref[...])
    emb = jnp.dot(p_ref[...].astype(BF16), wp_ref[...], preferred_element_type=F32)
    y_ref[...] = y + gate * emb


def _layer_tail(parts, w_outs, x2, g1, b1, w_up, w_down, g2, b2, w_gate, b_gate, p2, w_proj, tm, tf):
    m = x2.shape[0]
    row = lambda c: pl.BlockSpec((tm, c), lambda i: (i, 0))
    vec = _const_spec((1, D_MODEL))
    return pl.pallas_call(
        functools.partial(_tail_kernel, n_parts=len(parts), tf=tf),
        grid=(m // tm,),
        in_specs=[row(o.shape[1]) for o in parts] + [_const_spec(w.shape) for w in w_outs]
        + [row(D_MODEL), vec, vec, _const_spec((D_MODEL, D_FF)), _const_spec((D_FF, D_MODEL)),
           vec, vec, _const_spec((D_MODEL, D_MODEL)), vec, row(D_PLE),
           _const_spec((D_PLE, D_MODEL))],
        out_specs=row(D_MODEL),
        out_shape=jax.ShapeDtypeStruct((m, D_MODEL), F32),
        compiler_params=_cparams(1),
        name="layer_tail",
    )(*parts, *w_outs, x2, g1, b1, w_up, w_down, g2, b2, w_gate, b_gate, p2, w_proj)


def _pad_cols(w, n):
    return jnp.pad(w, ((0, 0), (0, n - w.shape[1])))


def _even_weights(w_in, q_norm, w_uq, kv_norm, w_ukv):
    sizes = [MLA_Q_LORA, MLA_KV_LORA, MLA_ROPE, SWA_HEADS * HEAD_DIM, SWA_KV_HEADS * HEAD_DIM]
    c_q, c_kv, k_r, q_s, k_s, v_s = jnp.split(w_in, np.cumsum(sizes).tolist(), axis=1)
    half = MLA_ROPE // 2
    rot = lambda wr: jnp.concatenate([-wr[..., half:], wr[..., :half]], axis=-1)
    d = w_in.shape[0]
    zeros = lambda n: jnp.zeros((d, n), w_in.dtype)
    kr_blk = jnp.concatenate([zeros(MLA_NOPE), k_r, zeros(LANES - MLA_NOPE - MLA_ROPE)], axis=1)
    krr_blk = jnp.concatenate([zeros(MLA_NOPE), rot(k_r), zeros(LANES - MLA_NOPE - MLA_ROPE)], axis=1)
    dup = lambda t: jnp.concatenate(
        [t[:, kv * HEAD_DIM:(kv + 1) * HEAD_DIM] for kv in range(SWA_KV_HEADS) for _ in range(2)], axis=1)
    w_in2 = jnp.concatenate([c_q, c_kv, kr_blk, krr_blk, q_s * (HEAD_DIM ** -0.5 * LOG2E), dup(k_s), dup(v_s)],
                            axis=1)
    assert w_in2.shape[1] == _E_END
    r = w_uq.shape[0]
    uq = w_uq.reshape(r, MLA_HEADS, MLA_NOPE + MLA_ROPE)
    zq = jnp.zeros((r, MLA_HEADS, MLA_PAD - MLA_NOPE - MLA_ROPE), w_uq.dtype)
    uq_pad = jnp.concatenate([uq, zq], axis=-1).reshape(r, MLA_HEADS * MLA_PAD)
    uq_rot = jnp.concatenate([jnp.zeros_like(uq[..., :MLA_NOPE]), rot(uq[..., MLA_NOPE:]), zq],
                             axis=-1).reshape(r, MLA_HEADS * MLA_PAD)
    rk = w_ukv.shape[0]
    ukv = w_ukv.reshape(rk, MLA_HEADS, MLA_NOPE + MLA_V)
    uk_pad = jnp.concatenate([ukv[..., :MLA_NOPE],
                              jnp.zeros((rk, MLA_HEADS, MLA_PAD - MLA_NOPE), w_ukv.dtype)],
                             axis=-1).reshape(rk, MLA_HEADS * MLA_PAD)
    uv = ukv[..., MLA_NOPE:].reshape(rk, MLA_HEADS * MLA_V)
    return dict(w_in=w_in2.astype(BF16), q_norm=q_norm.reshape(1, -1), kv_norm=kv_norm.reshape(1, -1),
                w_uq=uq_pad.astype(BF16), w_uqr=uq_rot.astype(BF16), w_uk=uk_pad.astype(BF16),
                w_uv=uv.astype(BF16))


def _rope_tables(seq):
    inv = 1.0 / (ROPE_THETA ** (jnp.arange(0, MLA_ROPE, 2, dtype=F32) / MLA_ROPE))
    ang_t = inv[:, None] * jnp.arange(seq, dtype=F32)[None, :]
    cos = jnp.concatenate([jnp.cos(ang_t)] * 2, axis=0).T
    sin = jnp.concatenate([jnp.sin(ang_t)] * 2, axis=0).T
    ones = jnp.ones((seq, MLA_NOPE), F32)
    z_n = jnp.zeros((seq, MLA_NOPE), F32)
    z_t = jnp.zeros((seq, MLA_PAD - MLA_NOPE - MLA_ROPE), F32)
    kc = jnp.concatenate([z_n, cos, z_t], axis=1)
    ks = jnp.concatenate([z_n, sin, z_t], axis=1)
    scale = (MLA_NOPE + MLA_ROPE) ** -0.5 * LOG2E
    qc = jnp.concatenate([ones, cos, z_t], axis=1) * scale
    qs = ks * scale
    return dict(qc=qc, qs=qs, kc=kc, ks=ks)


def _t5_bucket(dist):
    exact = REL_BUCKETS // 2
    d = jnp.maximum(dist, 1).astype(F32)
    large = exact + (jnp.log(d / exact) / math.log(REL_MAX_DIST / exact)
                     * (REL_BUCKETS - exact)).astype(jnp.int32)
    large = jnp.minimum(large, REL_BUCKETS - 1)
    return jnp.where(dist < exact, dist, large)


def _swa_bias_kernel(rel_ref, out_ref):
    a = lax.broadcasted_iota(jnp.int32, (BLOCK_Q, 2 * BLOCK_Q), 0)
    col = lax.broadcasted_iota(jnp.int32, (BLOCK_Q, 2 * BLOCK_Q), 1)
    for tab, shift in enumerate((BLOCK_Q, 0)):
        dist = a + shift - col
        valid = jnp.logical_and(dist >= 0, dist < SWA_WINDOW)
        bucket = _t5_bucket(jnp.maximum(dist, 0))
        for hd in range(SWA_HEADS):
            bias = jnp.zeros((BLOCK_Q, 2 * BLOCK_Q), F32)
            for bk in range(REL_BUCKETS):
                bias = jnp.where(bucket == bk, rel_ref[bk, hd], bias)
            out_ref[tab, hd] = jnp.where(valid, bias * LOG2E, NEG_INF)


def _swa_bias(rel_bias):
    return pl.pallas_call(
        _swa_bias_kernel,
        in_specs=[pl.BlockSpec(memory_space=pltpu.SMEM)],
        out_shape=jax.ShapeDtypeStruct((2, SWA_HEADS, BLOCK_Q, 2 * BLOCK_Q), F32),
        name="swa_bias_table",
    )(rel_bias)


def kernel(x, p, rel_bias, ev_w_in, ev_q_norm, ev_w_uq, ev_kv_norm, ev_w_ukv, ev_sinks, ev_w_out,
           od_w_in, od_b_f, od_w_out, ln1_g, ln1_b, w_up, w_down, ln2_g, ln2_b,
           ple_w_proj, ple_w_gate, ple_b_gate):
    b, seq, d = x.shape
    m = b * seq
    tm = min(512, seq)
    tm_proj = min(1024, seq)
    tile = min(512, seq)
    assert d == D_MODEL and seq % tm == 0 and seq % tile == 0 and seq % BLOCK_Q == 0
    assert seq >= 2 * BLOCK_Q
    tabs = _rope_tables(seq)
    bias = _swa_bias(rel_bias)
    x2 = x.reshape(m, d)
    hd = FOX_HEADS * HEAD_DIM
    row2 = lambda v: v.reshape(1, -1)
    for i in range(DEPTH):
        j = i // 2
        if i % 2 == 0:
            w = _even_weights(ev_w_in[j], ev_q_norm[j], ev_w_uq[j], ev_kv_norm[j], ev_w_ukv[j])
            qm, km, vm, qsw, ksw, vsw = _even_proj(x2, w, tabs, seq, tm_proj)
            r3 = lambda t: t.reshape(b, seq, t.shape[1])
            o_mla = _flash(r3(qm), r3(km), r3(vm), tile=tile)
            o_swa = _swa(r3(qsw), r3(ksw), r3(vsw), bias, ev_sinks[j])
            w_out = ev_w_out[j].astype(BF16)
            n_mla = MLA_HEADS * MLA_V
            parts = [o_mla.reshape(m, -1), o_swa.reshape(m, -1)]
            w_outs = [w_out[:n_mla], w_out[n_mla:]]
        else:
            wi = od_w_in[j]
            w = dict(wq=(wi[:, :hd] * (HEAD_DIM ** -0.5 * LOG2E)).astype(BF16),
                     wk=wi[:, hd:2 * hd].astype(BF16),
                     wv=wi[:, 2 * hd:3 * hd].astype(BF16),
                     wf=_pad_cols(wi[:, 3 * hd:], LANES).astype(BF16),
                     bf=_pad_cols(row2(od_b_f[j]), LANES))
            q, k, v, qx, kx = _odd_proj(x2, w, seq, tm_proj)
            r3 = lambda t: t.reshape(b, seq, t.shape[1])
            parts = [_flash(r3(q), r3(k), r3(v), r3(qx), r3(kx), tile=tile).reshape(m, -1)]
            w_outs = [od_w_out[j].astype(BF16)]
        x2 = _layer_tail(parts, w_outs, x2, row2(ln1_g[i]), row2(ln1_b[i]),
                         w_up[i].astype(BF16), w_down[i].astype(BF16), row2(ln2_g[i]), row2(ln2_b[i]),
                         ple_w_gate[i].astype(BF16), row2(ple_b_gate[i]), p[i].reshape(m, D_PLE),
                         ple_w_proj[i].astype(BF16), tm, min(1024, D_FF))
    return x2.reshape(b, seq, d)
```

```python
import functools
import math

import jax
import jax.numpy as jnp
import numpy as np
from jax import lax
from jax.experimental import pallas as pl
from jax.experimental.pallas import tpu as pltpu

D_MODEL = 1024
HEAD_DIM = 64
MLA_HEADS = 8
MLA_NOPE = 64
MLA_ROPE = 32
MLA_V = 64
MLA_Q_LORA = 384
MLA_KV_LORA = 256
ROPE_THETA = 10000.0
SWA_HEADS = 8
SWA_KV_HEADS = 2
SWA_WINDOW = 128
REL_BUCKETS = 32
REL_MAX_DIST = 128
FOX_HEADS = 16
D_FF = 4 * D_MODEL
D_PLE = 256
BLOCK_Q = 128
DEPTH = 4
DN_ALPHA = (2 * DEPTH) ** 0.25
NORM_EPS = 1e-5
NEG_INF = -1e30

LANES = 128
MLA_PAD = 128
VMEM_LIMIT = 52 * 1024 * 1024
LOG2E = math.log2(math.e)
FLASH_PAIRS_PER_STEP = 4
TAIL_ROW_GROUPS = 2
SWA_BLOCKS_PER_STEP = 8
DECAY_PARTS = 3
DECAY_LANES = 8

F32 = jnp.float32
BF16 = jnp.bfloat16

_E_CQ = 0
_E_CKV = _E_CQ + MLA_Q_LORA
_E_KR = _E_CKV + MLA_KV_LORA
_E_KRR = _E_KR + LANES
_E_QS = _E_KRR + LANES
_E_KS = _E_QS + SWA_HEADS * HEAD_DIM
_E_VS = _E_KS + 2 * SWA_KV_HEADS * HEAD_DIM
_E_END = _E_VS + 2 * SWA_KV_HEADS * HEAD_DIM


def _cparams(n_axes):
    return pltpu.CompilerParams(dimension_semantics=("arbitrary",) * n_axes,
                                vmem_limit_bytes=VMEM_LIMIT)


def _const_spec(shape):
    return pl.BlockSpec(shape, lambda *_: (0,) * len(shape), pipeline_mode=pl.Buffered(1))


def _layer_norm(y, g, b):
    mu = jnp.mean(y, axis=-1, keepdims=True)
    yc = y - mu
    var = jnp.mean(yc * yc, axis=-1, keepdims=True)
    return yc * lax.rsqrt(var + NORM_EPS) * g + b


def _rms_norm(y, g):
    return y * lax.rsqrt(jnp.mean(y * y, axis=-1, keepdims=True) + NORM_EPS) * g


def _even_proj_kernel(x_ref, w_in_ref, qn_ref, kvn_ref, w_uq_ref, w_uqr_ref, w_uk_ref, w_uv_ref,
                      qc_ref, qs_ref, kc_ref, ks_ref,
                      qm_ref, km_ref, vm_ref, qsw_ref, ksw_ref, vsw_ref):
    xb = x_ref[...].astype(BF16)
    h = jnp.dot(xb, w_in_ref[...], preferred_element_type=F32)
    cq = _rms_norm(h[:, _E_CQ:_E_CKV], qn_ref[...]).astype(BF16)
    ckv = _rms_norm(h[:, _E_CKV:_E_KR], kvn_ref[...]).astype(BF16)
    qa = jnp.dot(cq, w_uq_ref[...], preferred_element_type=F32)
    qb = jnp.dot(cq, w_uqr_ref[...], preferred_element_type=F32)
    qc = jnp.concatenate([qc_ref[...]] * MLA_HEADS, axis=1)
    qs = jnp.concatenate([qs_ref[...]] * MLA_HEADS, axis=1)
    qm_ref[...] = (qa * qc + qb * qs).astype(BF16)
    kr = h[:, _E_KR:_E_KRR] * kc_ref[...] + h[:, _E_KRR:_E_QS] * ks_ref[...]
    kn = jnp.dot(ckv, w_uk_ref[...], preferred_element_type=F32)
    for hd in range(MLA_HEADS):
        sl = slice(hd * MLA_PAD, (hd + 1) * MLA_PAD)
        km_ref[:, sl] = (kn[:, sl] + kr).astype(BF16)
    vm_ref[...] = jnp.dot(ckv, w_uv_ref[...], preferred_element_type=F32).astype(BF16)
    qsw_ref[...] = h[:, _E_QS:_E_KS].astype(BF16)
    ksw_ref[...] = h[:, _E_KS:_E_VS].astype(BF16)
    vsw_ref[...] = h[:, _E_VS:_E_END].astype(BF16)


def _even_proj(x2, w, tabs, seq, tm):
    m = x2.shape[0]
    n_pos = seq // tm
    row = lambda n: pl.BlockSpec((tm, n), lambda i: (i, 0))
    pos = lambda n: pl.BlockSpec((tm, n), lambda i: (i % n_pos, 0))
    hq = MLA_HEADS * MLA_PAD
    out_shape = (
        jax.ShapeDtypeStruct((m, hq), BF16), jax.ShapeDtypeStruct((m, hq), BF16),
        jax.ShapeDtypeStruct((m, MLA_HEADS * MLA_V), BF16),
        jax.ShapeDtypeStruct((m, SWA_HEADS * HEAD_DIM), BF16),
        jax.ShapeDtypeStruct((m, 2 * SWA_KV_HEADS * HEAD_DIM), BF16),
        jax.ShapeDtypeStruct((m, 2 * SWA_KV_HEADS * HEAD_DIM), BF16),
    )
    return pl.pallas_call(
        _even_proj_kernel,
        grid=(m // tm,),
        in_specs=[row(D_MODEL), _const_spec(w["w_in"].shape), _const_spec((1, MLA_Q_LORA)),
                  _const_spec((1, MLA_KV_LORA)), _const_spec(w["w_uq"].shape),
                  _const_spec(w["w_uqr"].shape), _const_spec(w["w_uk"].shape),
                  _const_spec(w["w_uv"].shape), pos(LANES), pos(LANES), pos(LANES), pos(LANES)],
        out_specs=(row(hq), row(hq), row(MLA_HEADS * MLA_V), row(SWA_HEADS * HEAD_DIM),
                   row(2 * SWA_KV_HEADS * HEAD_DIM), row(2 * SWA_KV_HEADS * HEAD_DIM)),
        out_shape=out_shape,
        compiler_params=_cparams(1),
        name="even_proj",
    )(x2, w["w_in"], w["q_norm"], w["kv_norm"], w["w_uq"], w["w_uqr"], w["w_uk"], w["w_uv"],
      tabs["qc"], tabs["qs"], tabs["kc"], tabs["ks"])


def _split3(c):
    hi = c.astype(BF16)
    r1 = c - hi.astype(F32)
    mid = r1.astype(BF16)
    lo = (r1 - mid.astype(F32)).astype(BF16)
    return hi, mid, lo


def _odd_proj_kernel(x_ref, wq_ref, wk_ref, wv_ref, wf_ref, bf_ref, place_ref, ones_ref,
                     q_ref, k_ref, v_ref, qx_ref, kx_ref, carry_ref, *, n_pos):
    tm = x_ref.shape[0]
    xb = x_ref[...].astype(BF16)
    q_ref[...] = jnp.dot(xb, wq_ref[...], preferred_element_type=F32).astype(BF16)
    k_ref[...] = jnp.dot(xb, wk_ref[...], preferred_element_type=F32).astype(BF16)
    v_ref[...] = jnp.dot(xb, wv_ref[...], preferred_element_type=F32).astype(BF16)
    z = jnp.dot(xb, wf_ref[...], preferred_element_type=F32) + bf_ref[...]
    logf = jnp.minimum(z, 0.0) - jnp.log1p(jnp.exp(-jnp.abs(z)))

    @pl.when(pl.program_id(0) % n_pos == 0)
    def _():
        carry_ref[...] = jnp.zeros_like(carry_ref)

    rows = lax.broadcasted_iota(jnp.int32, (tm, LANES), 0)
    c = logf
    sh = 1
    while sh < tm:
        c = c + jnp.where(rows >= sh, pltpu.roll(c, sh, 0), 0.0)
        sh *= 2
    c = c + carry_ref[0:1, :]
    carry_ref[...] = jnp.broadcast_to(c[tm - 1:tm, :], carry_ref.shape)
    parts = jnp.concatenate(_split3(c * LOG2E), axis=1)
    placed = jnp.dot(parts, place_ref[...], preferred_element_type=F32) + ones_ref[...]
    qx_ref[...] = placed[:, :LANES].astype(BF16)
    kx_ref[...] = placed[:, LANES:].astype(BF16)


def _odd_proj(x2, w, seq, tm):
    m = x2.shape[0]
    n_pos = seq // tm
    row = lambda n: pl.BlockSpec((tm, n), lambda i: (i, 0))
    hd = FOX_HEADS * HEAD_DIM
    out_shape = (jax.ShapeDtypeStruct((m, hd), BF16),) * 3 + (jax.ShapeDtypeStruct((m, LANES), BF16),) * 2
    place, ones = _decay_placement()
    return pl.pallas_call(
        functools.partial(_odd_proj_kernel, n_pos=n_pos),
        grid=(m // tm,),
        in_specs=[row(D_MODEL), _const_spec((D_MODEL, hd)), _const_spec((D_MODEL, hd)),
                  _const_spec((D_MODEL, hd)), _const_spec((D_MODEL, LANES)), _const_spec((1, LANES)),
                  _const_spec(place.shape), _const_spec(ones.shape)],
        out_specs=(row(hd), row(hd), row(hd), row(LANES), row(LANES)),
        out_shape=out_shape,
        scratch_shapes=[pltpu.VMEM((8, LANES), F32)],
        compiler_params=_cparams(1),
        name="odd_proj",
    )(x2, w["wq"], w["wk"], w["wv"], w["wf"], w["bf"], place, ones)


def _decay_placement():
    place = np.zeros((DECAY_PARTS * LANES, 2 * LANES), np.float32)
    ones = np.zeros((1, 2 * LANES), np.float32)
    for h in range(FOX_HEADS):
        for t in range(DECAY_PARTS):
            place[t * LANES + h, DECAY_LANES * h + t] = 1.0
            place[t * LANES + h, LANES + DECAY_LANES * h + DECAY_PARTS + t] = -1.0
            ones[0, DECAY_LANES * h + DECAY_PARTS + t] = 1.0
            ones[0, LANES + DECAY_LANES * h + t] = 1.0
    return jnp.asarray(place, BF16), jnp.asarray(ones, F32)


def _flash_kernel(*refs, tile, seq, dqk, fox, pairs):
    if fox:
        q_ref, k_ref, v_ref, qx_ref, kx_ref, o_ref, s_ref, mc_ref, m_ref, acc_ref, mask_ref = refs
    else:
        q_ref, k_ref, v_ref, o_ref, s_ref, mc_ref, m_ref, acc_ref, mask_ref = refs
    nq = seq // tile
    n_steps = nq * (nq + 1) // 2
    first_pair = pl.program_id(1) * pairs
    lane = lax.broadcasted_iota(jnp.int32, (1, LANES), 1)
    lo = lane < HEAD_DIM
    reps = tile // LANES
    nt = (((1,), (1,)), ((), ()))
    lane_full = lax.broadcasted_iota(jnp.int32, (tile, LANES), 1)
    ones_lo = jnp.where(lane_full < HEAD_DIM, 1.0, 0.0).astype(BF16)
    ones_hi = jnp.where(lane_full < HEAD_DIM, 0.0, 1.0).astype(BF16)

    def q_operands(qi):
        q0 = pl.multiple_of(qi * tile, tile)
        qa = []
        for pp in range(pairs):
            q_pair = q_ref[0, pl.ds(q0, tile), pp * dqk:(pp + 1) * dqk]
            if fox:
                zero = jnp.zeros_like(q_pair)
                qx_blk = qx_ref[0, pl.ds(q0, tile), :]
                for hh in range(2):
                    qm = jnp.where(lo, q_pair, zero) if hh == 0 else jnp.where(lo, zero, q_pair)
                    head = 2 * (first_pair + pp) + hh
                    qx = jnp.where(lane // DECAY_LANES == head, qx_blk, jnp.zeros_like(qx_blk))
                    qa.append(jnp.concatenate([qm, qx], axis=1))
            else:
                qa += [q_pair[:, :LANES], q_pair[:, LANES:]]
        return qa

    def stage_a(qi, j):
        qa = q_operands(qi)
        k0 = pl.multiple_of(j * tile, tile)
        mask = mask_ref[jnp.where(j == qi, 1, 0)]
        for pp in range(pairs):
            k_pair = k_ref[0, pl.ds(k0, tile), pp * dqk:(pp + 1) * dqk]
            if fox:
                k_pair = jnp.concatenate([k_pair, kx_ref[0, pl.ds(k0, tile), :]], axis=1)
            for hh in range(2):
                h = 2 * pp + hh
                kh = k_pair if fox else k_pair[:, hh * LANES:(hh + 1) * LANES]
                s = lax.dot_general(qa[h], kh, nt, preferred_element_type=F32)
                s = s + mask
                s_ref[h] = s
                mc_ref[h] = jnp.broadcast_to(jnp.max(s, axis=1, keepdims=True), (tile, LANES))

    def stage_b(j):
        k0 = pl.multiple_of(j * tile, tile)
        m_cap = jnp.where(j == 0, NEG_INF, -NEG_INF)
        for pp in range(pairs):
            v_pair = v_ref[0, pl.ds(k0, tile), pp * LANES:(pp + 1) * LANES]
            zero = jnp.zeros_like(v_pair)
            v_stack = jnp.concatenate(
                [jnp.concatenate([jnp.where(lo, v_pair, zero), ones_lo], axis=1),
                 jnp.concatenate([jnp.where(lo, zero, v_pair), ones_hi], axis=1)],
                axis=0)
            ps = []
            alphas = []
            for hh in range(2):
                h = 2 * pp + hh
                m_prev = jnp.minimum(m_ref[h], m_cap)
                m_new = jnp.maximum(m_prev, mc_ref[h])
                alphas.append(jnp.exp2(m_prev - m_new))
                ps.append(jnp.exp2(s_ref[h] - jnp.concatenate([m_new] * reps, axis=1)).astype(BF16))
                m_ref[h] = m_new
            pv = jnp.dot(jnp.concatenate(ps, axis=1), v_stack, preferred_element_type=F32)
            alpha = jnp.where(lo, alphas[0], alphas[1])
            acc_ref[pp] = acc_ref[pp] * jnp.concatenate([alpha, alpha], axis=1) + pv

    def finalize(qi):
        q0 = pl.multiple_of(qi * tile, tile)
        for pp in range(pairs):
            a = acc_ref[pp]
            o_ref[0, pl.ds(q0, tile), pp * LANES:(pp + 1) * LANES] = (
                a[:, :LANES] / a[:, LANES:]).astype(o_ref.dtype)

    m_ref[...] = jnp.zeros(m_ref.shape, F32)
    acc_ref[...] = jnp.zeros(acc_ref.shape, F32)
    dcol = (lax.broadcasted_iota(jnp.int32, (tile, tile), 1)
            - lax.broadcasted_iota(jnp.int32, (tile, tile), 0))
    mask_ref[0] = jnp.zeros((tile, tile), F32)
    mask_ref[1] = jnp.where(dcol <= 0, 0.0, NEG_INF)
    stage_a(0, 0)

    def body(t, carry):
        qi, j = carry
        last = j == qi
        qn = jnp.where(last, qi + 1, qi)
        jn = jnp.where(last, 0, j + 1)
        stage_b(j)
        stage_a(qn, jn)

        @pl.when(last)
        def _():
            finalize(qi)

        return qn, jn

    lax.fori_loop(0, n_steps - 1, body, (jnp.int32(0), jnp.int32(0)))
    stage_b(nq - 1)
    finalize(nq - 1)


def _flash(q, k, v, qx=None, kx=None, *, tile):
    b, seq, _ = q.shape
    all_pairs = v.shape[2] // LANES
    dqk = q.shape[2] // all_pairs
    fox = qx is not None
    pairs = min(FLASH_PAIRS_PER_STEP, all_pairs)
    heads = 2 * pairs
    assert all_pairs % pairs == 0
    blk = lambda n: pl.BlockSpec((1, seq, pairs * n), lambda bi, g: (bi, 0, g))
    in_specs = [blk(dqk), blk(dqk), blk(LANES)]
    args = [q, k, v]
    if fox:
        in_specs += [pl.BlockSpec((1, seq, LANES), lambda bi, g: (bi, 0, 0))] * 2
        args += [qx, kx]
    scratch = [pltpu.VMEM((heads, tile, tile), F32), pltpu.VMEM((heads, tile, LANES), F32),
               pltpu.VMEM((heads, tile, LANES), F32), pltpu.VMEM((pairs, tile, 2 * LANES), F32),
               pltpu.VMEM((2, tile, tile), F32)]
    return pl.pallas_call(
        functools.partial(_flash_kernel, tile=tile, seq=seq, dqk=dqk, fox=fox, pairs=pairs),
        grid=(b, all_pairs // pairs),
        in_specs=in_specs,
        out_specs=blk(LANES),
        out_shape=jax.ShapeDtypeStruct((b, seq, all_pairs * LANES), BF16),
        scratch_shapes=scratch,
        compiler_params=_cparams(2),
        name="fox_attn" if fox else "mla_attn",
    )(*args)


def _swa_kernel(q_ref, k_ref, v_ref, bias_ref, sink_ref, o_ref, *, seq):
    nb = seq // BLOCK_Q
    lane = lax.broadcasted_iota(jnp.int32, (1, LANES), 1)
    lo = lane < HEAD_DIM
    group = SWA_HEADS // SWA_KV_HEADS
    band = 2 * BLOCK_Q
    ones_v = jnp.ones((band, LANES), BF16)
    nt = (((1,), (1,)), ((), ()))

    def block(n):
        r0 = pl.multiple_of(n * BLOCK_Q, BLOCK_Q)
        b0 = pl.multiple_of(jnp.maximum(n - 1, 0) * BLOCK_Q, BLOCK_Q)
        tab = jnp.where(n == 0, 1, 0)
        for pair in range(SWA_HEADS // 2):
            kvh = (2 * pair) // group
            ksl = slice(kvh * LANES, (kvh + 1) * LANES)
            q_pair = q_ref[0, pl.ds(r0, BLOCK_Q), pair * LANES:(pair + 1) * LANES]
            k_band = k_ref[0, pl.ds(b0, band), ksl]
            va = jnp.concatenate([v_ref[0, pl.ds(b0, band), ksl], ones_v], axis=1)
            zq = jnp.zeros_like(q_pair)
            res = []
            for hh in range(2):
                hd = 2 * pair + hh
                qh = jnp.where(lo, q_pair, zq) if hh == 0 else jnp.where(lo, zq, q_pair)
                s = lax.dot_general(qh, k_band, nt, preferred_element_type=F32) + bias_ref[tab, hd]
                sink = sink_ref[hd] * LOG2E
                mx = jnp.maximum(jnp.max(s, axis=1, keepdims=True), sink)
                p = jnp.exp2(s - mx).astype(BF16)
                pv = jnp.dot(p, va, preferred_element_type=F32)
                den = pv[:, LANES:] + jnp.exp2(sink - mx)
                res.append(pv[:, :LANES] / den)
            o_ref[0, pl.ds(r0, BLOCK_Q), pair * LANES:(pair + 1) * LANES] = (
                jnp.where(lo, res[0], res[1]).astype(o_ref.dtype))

    unroll = SWA_BLOCKS_PER_STEP if nb % SWA_BLOCKS_PER_STEP == 0 else 1

    def body(i, carry):
        for u in range(unroll):
            block(i * unroll + u)
        return carry

    lax.fori_loop(0, nb // unroll, body, 0)


def _swa(q, k, v, bias, sinks):
    b, seq, dq = q.shape
    dk = k.shape[2]
    return pl.pallas_call(
        functools.partial(_swa_kernel, seq=seq),
        grid=(b,),
        in_specs=[pl.BlockSpec((1, seq, dq), lambda bi: (bi, 0, 0)),
                  pl.BlockSpec((1, seq, dk), lambda bi: (bi, 0, 0)),
                  pl.BlockSpec((1, seq, dk), lambda bi: (bi, 0, 0)),
                  _const_spec(bias.shape),
                  pl.BlockSpec(memory_space=pltpu.SMEM)],
        out_specs=pl.BlockSpec((1, seq, dq), lambda bi: (bi, 0, 0)),
        out_shape=jax.ShapeDtypeStruct((b, seq, dq), BF16),
        compiler_params=_cparams(1),
        name="swa_attn",
    )(q, k, v, bias, sinks)


def _tail_kernel(*refs, n_parts, tf):
    o_refs = refs[:n_parts]
    w_refs = refs[n_parts:2 * n_parts]
    (x_ref, g1_ref, b1_ref, wu_ref, wd_ref, g2_ref, b2_ref, wg_ref, bg_ref, p_ref, wp_ref,
     y_ref) = refs[2 * n_parts:]
    tm = x_ref.shape[0]
    groups = [slice(r * (tm // TAIL_ROW_GROUPS), (r + 1) * (tm // TAIL_ROW_GROUPS))
              for r in range(TAIL_ROW_GROUPS)]
    mixes = []
    for rows in groups:
        mix = None
        for o_ref, w_ref in zip(o_refs, w_refs):
            d = jnp.dot(o_ref[rows, :], w_ref[...], preferred_element_type=F32)
            mix = d if mix is None else mix + d
        mixes.append(mix)
    x1s = [_layer_norm(DN_ALPHA * x_ref[rows, :] + mix, g1_ref[...], b1_ref[...])
           for rows, mix in zip(groups, mixes)]
    zs = []
    for x1 in x1s:
        xb = x1.astype(BF16)
        acc = None
        for f in range(D_FF // tf):
            hdn = jnp.maximum(
                jnp.dot(xb, wu_ref[:, f * tf:(f + 1) * tf], preferred_element_type=F32), 0.0)
            part = jnp.dot((hdn * hdn).astype(BF16), wd_ref[f * tf:(f + 1) * tf, :],
                           preferred_element_type=F32)
            acc = part if acc is None else acc + part
        zs.append(DN_ALPHA * x1 + acc)
    for rows, z in zip(groups, zs):
        y = _layer_norm(z, g2_ref[...], b2_ref[...])
        gate = jax.nn.sigmoid(
            jnp.dot(y.astype(BF16), wg_ref[...], preferred_element_type=F32) + bg_ref[...])
        emb = jnp.dot(p_ref[0, rows, :].astype(BF16), wp_ref[...], preferred_element_type=F32)
        y_ref[rows, :] = y + gate * emb


def _layer_tail(parts, w_outs, x2, g1, b1, w_up, w_down, g2, b2, w_gate, b_gate, p3, layer, w_proj,
                tm, tf):
    m = x2.shape[0]
    row = lambda c: pl.BlockSpec((tm, c), lambda i: (i, 0))
    p_spec = pl.BlockSpec((1, tm, D_PLE), lambda i: (layer, i, 0))
    vec = _const_spec((1, D_MODEL))
    return pl.pallas_call(
        functools.partial(_tail_kernel, n_parts=len(parts), tf=tf),
        grid=(m // tm,),
        in_specs=[row(o.shape[1]) for o in parts] + [_const_spec(w.shape) for w in w_outs]
        + [row(D_MODEL), vec, vec, _const_spec((D_MODEL, D_FF)), _const_spec((D_FF, D_MODEL)),
           vec, vec, _const_spec((D_MODEL, D_MODEL)), vec, p_spec,
           _const_spec((D_PLE, D_MODEL))],
        out_specs=row(D_MODEL),
        out_shape=jax.ShapeDtypeStruct((m, D_MODEL), F32),
        compiler_params=_cparams(1),
        name="layer_tail",
    )(*parts, *w_outs, x2, g1, b1, w_up, w_down, g2, b2, w_gate, b_gate, p3, w_proj)


def _pad_cols(w, n):
    return jnp.pad(w, ((0, 0), (0, n - w.shape[1])))


def _even_weights(w_in, q_norm, w_uq, kv_norm, w_ukv):
    sizes = [MLA_Q_LORA, MLA_KV_LORA, MLA_ROPE, SWA_HEADS * HEAD_DIM, SWA_KV_HEADS * HEAD_DIM]
    c_q, c_kv, k_r, q_s, k_s, v_s = jnp.split(w_in, np.cumsum(sizes).tolist(), axis=1)
    half = MLA_ROPE // 2
    rot = lambda wr: jnp.concatenate([-wr[..., half:], wr[..., :half]], axis=-1)
    d = w_in.shape[0]
    zeros = lambda n: jnp.zeros((d, n), w_in.dtype)
    kr_blk = jnp.concatenate([zeros(MLA_NOPE), k_r, zeros(LANES - MLA_NOPE - MLA_ROPE)], axis=1)
    krr_blk = jnp.concatenate([zeros(MLA_NOPE), rot(k_r), zeros(LANES - MLA_NOPE - MLA_ROPE)], axis=1)
    dup = lambda t: jnp.concatenate(
        [t[:, kv * HEAD_DIM:(kv + 1) * HEAD_DIM] for kv in range(SWA_KV_HEADS) for _ in range(2)], axis=1)
    w_in2 = jnp.concatenate([c_q, c_kv, kr_blk, krr_blk, q_s * (HEAD_DIM ** -0.5 * LOG2E), dup(k_s), dup(v_s)],
                            axis=1)
    assert w_in2.shape[1] == _E_END
    r = w_uq.shape[0]
    uq = w_uq.reshape(r, MLA_HEADS, MLA_NOPE + MLA_ROPE)
    zq = jnp.zeros((r, MLA_HEADS, MLA_PAD - MLA_NOPE - MLA_ROPE), w_uq.dtype)
    uq_pad = jnp.concatenate([uq, zq], axis=-1).reshape(r, MLA_HEADS * MLA_PAD)
    uq_rot = jnp.concatenate([jnp.zeros_like(uq[..., :MLA_NOPE]), rot(uq[..., MLA_NOPE:]), zq],
                             axis=-1).reshape(r, MLA_HEADS * MLA_PAD)
    rk = w_ukv.shape[0]
    ukv = w_ukv.reshape(rk, MLA_HEADS, MLA_NOPE + MLA_V)
    uk_pad = jnp.concatenate([ukv[..., :MLA_NOPE],
                              jnp.zeros((rk, MLA_HEADS, MLA_PAD - MLA_NOPE), w_ukv.dtype)],
                             axis=-1).reshape(rk, MLA_HEADS * MLA_PAD)
    uv = ukv[..., MLA_NOPE:].reshape(rk, MLA_HEADS * MLA_V)
    return dict(w_in=w_in2.astype(BF16), q_norm=q_norm.reshape(1, -1), kv_norm=kv_norm.reshape(1, -1),
                w_uq=uq_pad.astype(BF16), w_uqr=uq_rot.astype(BF16), w_uk=uk_pad.astype(BF16),
                w_uv=uv.astype(BF16))


def _rope_tables(seq):
    inv = 1.0 / (ROPE_THETA ** (jnp.arange(0, MLA_ROPE, 2, dtype=F32) / MLA_ROPE))
    ang_t = inv[:, None] * jnp.arange(seq, dtype=F32)[None, :]
    cos = jnp.concatenate([jnp.cos(ang_t)] * 2, axis=0).T
    sin = jnp.concatenate([jnp.sin(ang_t)] * 2, axis=0).T
    ones = jnp.ones((seq, MLA_NOPE), F32)
    z_n = jnp.zeros((seq, MLA_NOPE), F32)
    z_t = jnp.zeros((seq, MLA_PAD - MLA_NOPE - MLA_ROPE), F32)
    kc = jnp.concatenate([z_n, cos, z_t], axis=1)
    ks = jnp.concatenate([z_n, sin, z_t], axis=1)
    scale = (MLA_NOPE + MLA_ROPE) ** -0.5 * LOG2E
    qc = jnp.concatenate([ones, cos, z_t], axis=1) * scale
    qs = ks * scale
    return dict(qc=qc, qs=qs, kc=kc, ks=ks)


def _t5_bucket(dist):
    exact = REL_BUCKETS // 2
    d = jnp.maximum(dist, 1).astype(F32)
    large = exact + (jnp.log(d / exact) / math.log(REL_MAX_DIST / exact)
                     * (REL_BUCKETS - exact)).astype(jnp.int32)
    large = jnp.minimum(large, REL_BUCKETS - 1)
    return jnp.where(dist < exact, dist, large)


def _swa_bias_kernel(rel_ref, out_ref):
    a = lax.broadcasted_iota(jnp.int32, (BLOCK_Q, 2 * BLOCK_Q), 0)
    col = lax.broadcasted_iota(jnp.int32, (BLOCK_Q, 2 * BLOCK_Q), 1)
    for tab, shift in enumerate((BLOCK_Q, 0)):
        dist = a + shift - col
        valid = jnp.logical_and(dist >= 0, dist < SWA_WINDOW)
        bucket = _t5_bucket(jnp.maximum(dist, 0))
        for hd in range(SWA_HEADS):
            bias = jnp.zeros((BLOCK_Q, 2 * BLOCK_Q), F32)
            for bk in range(REL_BUCKETS):
                bias = jnp.where(bucket == bk, rel_ref[bk, hd], bias)
            out_ref[tab, hd] = jnp.where(valid, bias * LOG2E, NEG_INF)


def _swa_bias(rel_bias):
    return pl.pallas_call(
        _swa_bias_kernel,
        in_specs=[pl.BlockSpec(memory_space=pltpu.SMEM)],
        out_shape=jax.ShapeDtypeStruct((2, SWA_HEADS, BLOCK_Q, 2 * BLOCK_Q), F32),
        name="swa_bias_table",
    )(rel_bias)


def kernel(x, p, rel_bias, ev_w_in, ev_q_norm, ev_w_uq, ev_kv_norm, ev_w_ukv, ev_sinks, ev_w_out,
           od_w_in, od_b_f, od_w_out, ln1_g, ln1_b, w_up, w_down, ln2_g, ln2_b,
           ple_w_proj, ple_w_gate, ple_b_gate):
    b, seq, d = x.shape
    m = b * seq
    tm = min(512, seq)
    tm_proj = min(1024, seq)
    tile = min(512, seq)
    assert d == D_MODEL and seq % tm == 0 and seq % tile == 0 and seq % BLOCK_Q == 0
    assert seq >= 2 * BLOCK_Q
    tabs = _rope_tables(seq)
    bias = _swa_bias(rel_bias)
    x2 = x.reshape(m, d)
    p3 = p.reshape(p.shape[0], m, D_PLE)
    hd = FOX_HEADS * HEAD_DIM
    row2 = lambda v: v.reshape(1, -1)
    for i in range(DEPTH):
        j = i // 2
        if i % 2 == 0:
            w = _even_weights(ev_w_in[j], ev_q_norm[j], ev_w_uq[j], ev_kv_norm[j], ev_w_ukv[j])
            qm, km, vm, qsw, ksw, vsw = _even_proj(x2, w, tabs, seq, tm_proj)
            r3 = lambda t: t.reshape(b, seq, t.shape[1])
            o_mla = _flash(r3(qm), r3(km), r3(vm), tile=tile)
            o_swa = _swa(r3(qsw), r3(ksw), r3(vsw), bias, ev_sinks[j])
            w_out = ev_w_out[j].astype(BF16)
            n_mla = MLA_HEADS * MLA_V
            parts = [o_mla.reshape(m, -1), o_swa.reshape(m, -1)]
            w_outs = [w_out[:n_mla], w_out[n_mla:]]
        else:
            wi = od_w_in[j]
            w = dict(wq=(wi[:, :hd] * (HEAD_DIM ** -0.5 * LOG2E)).astype(BF16),
                     wk=wi[:, hd:2 * hd].astype(BF16),
                     wv=wi[:, 2 * hd:3 * hd].astype(BF16),
                     wf=_pad_cols(wi[:, 3 * hd:], LANES).astype(BF16),
                     bf=_pad_cols(row2(od_b_f[j]), LANES))
            q, k, v, qx, kx = _odd_proj(x2, w, seq, tm_proj)
            r3 = lambda t: t.reshape(b, seq, t.shape[1])
            parts = [_flash(r3(q), r3(k), r3(v), r3(qx), r3(kx), tile=tile).reshape(m, -1)]
            w_outs = [od_w_out[j].astype(BF16)]
        x2 = _layer_tail(parts, w_outs, x2, row2(ln1_g[i]), row2(ln1_b[i]),
                         w_up[i].astype(BF16), w_down[i].astype(BF16), row2(ln2_g[i]), row2(ln2_b[i]),
                         ple_w_gate[i].astype(BF16), row2(ple_b_gate[i]), p3, i,
                         ple_w_proj[i].astype(BF16), tm, min(1024, D_FF))
    return x2.reshape(b, seq, d)
```

```python
import functools
import math

import jax
import jax.numpy as jnp
import numpy as np
from jax import lax
from jax.experimental import pallas as pl
from jax.experimental.pallas import tpu as pltpu

D_MODEL = 1024
HEAD_DIM = 64
MLA_HEADS = 8
MLA_NOPE = 64
MLA_ROPE = 32
MLA_V = 64
MLA_Q_LORA = 384
MLA_KV_LORA = 256
ROPE_THETA = 10000.0
SWA_HEADS = 8
SWA_KV_HEADS = 2
SWA_WINDOW = 128
REL_BUCKETS = 32
REL_MAX_DIST = 128
FOX_HEADS = 16
D_FF = 4 * D_MODEL
D_PLE = 256
BLOCK_Q = 128
DEPTH = 4
DN_ALPHA = (2 * DEPTH) ** 0.25
NORM_EPS = 1e-5
NEG_INF = -1e30

LANES = 128
MLA_PAD = 128
VMEM_LIMIT = 52 * 1024 * 1024
LOG2E = math.log2(math.e)
FLASH_PAIRS_PER_STEP = 4
TAIL_ROW_GROUPS = 2
SWA_BLOCKS_PER_STEP = 8
DECAY_PARTS = 3
DECAY_LANES = 8

F32 = jnp.float32
BF16 = jnp.bfloat16

_E_CQ = 0
_E_CKV = _E_CQ + MLA_Q_LORA
_E_KR = _E_CKV + MLA_KV_LORA
_E_QS = _E_KR + LANES
_E_KS = _E_QS + SWA_HEADS * HEAD_DIM
_E_VS = _E_KS + SWA_KV_HEADS * HEAD_DIM
_E_END = _E_VS + SWA_KV_HEADS * HEAD_DIM
assert SWA_KV_HEADS * HEAD_DIM == LANES


def _cparams(n_axes):
    return pltpu.CompilerParams(dimension_semantics=("arbitrary",) * n_axes,
                                vmem_limit_bytes=VMEM_LIMIT)


def _const_spec(shape):
    return pl.BlockSpec(shape, lambda *_: (0,) * len(shape), pipeline_mode=pl.Buffered(1))


def _layer_norm(y, g, b):
    mu = jnp.mean(y, axis=-1, keepdims=True)
    yc = y - mu
    var = jnp.mean(yc * yc, axis=-1, keepdims=True)
    return yc * lax.rsqrt(var + NORM_EPS) * g + b


def _rms_norm(y, g):
    return y * lax.rsqrt(jnp.mean(y * y, axis=-1, keepdims=True) + NORM_EPS) * g


def _rope_block(y, c_ref, s_up_ref, s_dn_ref):
    half = MLA_ROPE // 2
    return (y * c_ref[...] + pltpu.roll(y, half, 1) * s_up_ref[...]
            + pltpu.roll(y, LANES - half, 1) * s_dn_ref[...])


def _even_proj_kernel(x_ref, w_in_ref, qn_ref, kvn_ref, w_uq_ref, w_uk_ref, w_uv_ref,
                      qc_ref, qsu_ref, qsd_ref, kc_ref, ksu_ref, ksd_ref,
                      qm_ref, km_ref, vm_ref, qsw_ref, ksw_ref, vsw_ref):
    lo = lax.broadcasted_iota(jnp.int32, (1, LANES), 1) < HEAD_DIM
    xb = x_ref[...].astype(BF16)
    h = jnp.dot(xb, w_in_ref[:, :_E_QS], preferred_element_type=F32)
    cq = _rms_norm(h[:, _E_CQ:_E_CKV], qn_ref[...]).astype(BF16)
    ckv = _rms_norm(h[:, _E_CKV:_E_KR], kvn_ref[...]).astype(BF16)
    qa = jnp.dot(cq, w_uq_ref[...], preferred_element_type=F32)
    for hd in range(MLA_HEADS):
        sl = slice(hd * MLA_PAD, (hd + 1) * MLA_PAD)
        qm_ref[:, sl] = _rope_block(qa[:, sl], qc_ref, qsu_ref, qsd_ref).astype(BF16)
    hs = jnp.dot(xb, w_in_ref[:, _E_QS:], preferred_element_type=F32)
    kr = _rope_block(h[:, _E_KR:_E_QS], kc_ref, ksu_ref, ksd_ref)
    kn = jnp.dot(ckv, w_uk_ref[...], preferred_element_type=F32)
    for hd in range(MLA_HEADS):
        sl = slice(hd * MLA_PAD, (hd + 1) * MLA_PAD)
        km_ref[:, sl] = (kn[:, sl] + kr).astype(BF16)
    vm_ref[...] = jnp.dot(ckv, w_uv_ref[...], preferred_element_type=F32).astype(BF16)
    n_q = _E_KS - _E_QS
    qsw_ref[...] = hs[:, :n_q].astype(BF16)
    for src, dst_ref in ((hs[:, n_q:n_q + LANES], ksw_ref), (hs[:, n_q + LANES:], vsw_ref)):
        swapped = pltpu.roll(src, HEAD_DIM, 1)
        dst_ref[:, :LANES] = jnp.where(lo, src, swapped).astype(BF16)
        dst_ref[:, LANES:] = jnp.where(lo, swapped, src).astype(BF16)


def _even_proj(x2, w, tabs, seq, tm):
    m = x2.shape[0]
    n_pos = seq // tm
    row = lambda n: pl.BlockSpec((tm, n), lambda i: (i, 0))
    pos = lambda n: pl.BlockSpec((tm, n), lambda i: (i % n_pos, 0))
    hq = MLA_HEADS * MLA_PAD
    out_shape = (
        jax.ShapeDtypeStruct((m, hq), BF16), jax.ShapeDtypeStruct((m, hq), BF16),
        jax.ShapeDtypeStruct((m, MLA_HEADS * MLA_V), BF16),
        jax.ShapeDtypeStruct((m, SWA_HEADS * HEAD_DIM), BF16),
        jax.ShapeDtypeStruct((m, 2 * SWA_KV_HEADS * HEAD_DIM), BF16),
        jax.ShapeDtypeStruct((m, 2 * SWA_KV_HEADS * HEAD_DIM), BF16),
    )
    return pl.pallas_call(
        _even_proj_kernel,
        grid=(m // tm,),
        in_specs=[row(D_MODEL), _const_spec(w["w_in"].shape), _const_spec((1, MLA_Q_LORA)),
                  _const_spec((1, MLA_KV_LORA)), _const_spec(w["w_uq"].shape),
                  _const_spec(w["w_uk"].shape), _const_spec(w["w_uv"].shape)] + [pos(LANES)] * 6,
        out_specs=(row(hq), row(hq), row(MLA_HEADS * MLA_V), row(SWA_HEADS * HEAD_DIM),
                   row(2 * SWA_KV_HEADS * HEAD_DIM), row(2 * SWA_KV_HEADS * HEAD_DIM)),
        out_shape=out_shape,
        compiler_params=_cparams(1),
        name="even_proj",
    )(x2, w["w_in"], w["q_norm"], w["kv_norm"], w["w_uq"], w["w_uk"], w["w_uv"],
      tabs["qc"], tabs["qsu"], tabs["qsd"], tabs["kc"], tabs["ksu"], tabs["ksd"])


def _split3(c):
    hi = c.astype(BF16)
    r1 = c - hi.astype(F32)
    mid = r1.astype(BF16)
    lo = (r1 - mid.astype(F32)).astype(BF16)
    return hi, mid, lo


def _odd_proj_kernel(x_ref, wq_ref, wk_ref, wv_ref, wf_ref, bf_ref, place_ref, ones_ref,
                     q_ref, k_ref, v_ref, qx_ref, kx_ref, carry_ref, *, n_pos):
    tm = x_ref.shape[0]

    @pl.when(pl.program_id(0) % n_pos == 0)
    def _():
        carry_ref[...] = jnp.zeros_like(carry_ref)

    xb = x_ref[...].astype(BF16)
    z = jnp.dot(xb, wf_ref[...], preferred_element_type=F32) + bf_ref[...]
    logf = jnp.minimum(z, 0.0) - jnp.log1p(jnp.exp(-jnp.abs(z)))
    q_ref[...] = jnp.dot(xb, wq_ref[...], preferred_element_type=F32).astype(BF16)
    k_ref[...] = jnp.dot(xb, wk_ref[...], preferred_element_type=F32).astype(BF16)
    v_ref[...] = jnp.dot(xb, wv_ref[...], preferred_element_type=F32).astype(BF16)
    rows = lax.broadcasted_iota(jnp.int32, (tm, LANES), 0)
    c = logf
    sh = 1
    while sh < tm:
        c = c + jnp.where(rows >= sh, pltpu.roll(c, sh, 0), 0.0)
        sh *= 2
    c = c + carry_ref[0:1, :]
    carry_ref[...] = jnp.broadcast_to(c[tm - 1:tm, :], carry_ref.shape)
    parts = jnp.concatenate(_split3(c * LOG2E), axis=1)
    placed = jnp.dot(parts, place_ref[...], preferred_element_type=F32) + ones_ref[...]
    qx_ref[...] = placed[:, :LANES].astype(BF16)
    kx_ref[...] = placed[:, LANES:].astype(BF16)


def _odd_proj(x2, w, seq, tm):
    m = x2.shape[0]
    n_pos = seq // tm
    row = lambda n: pl.BlockSpec((tm, n), lambda i: (i, 0))
    hd = FOX_HEADS * HEAD_DIM
    out_shape = (jax.ShapeDtypeStruct((m, hd), BF16),) * 3 + (jax.ShapeDtypeStruct((m, LANES), BF16),) * 2
    place, ones = _decay_placement()
    return pl.pallas_call(
        functools.partial(_odd_proj_kernel, n_pos=n_pos),
        grid=(m // tm,),
        in_specs=[row(D_MODEL), _const_spec((D_MODEL, hd)), _const_spec((D_MODEL, hd)),
                  _const_spec((D_MODEL, hd)), _const_spec((D_MODEL, LANES)), _const_spec((1, LANES)),
                  _const_spec(place.shape), _const_spec(ones.shape)],
        out_specs=(row(hd), row(hd), row(hd), row(LANES), row(LANES)),
        out_shape=out_shape,
        scratch_shapes=[pltpu.VMEM((8, LANES), F32)],
        compiler_params=_cparams(1),
        name="odd_proj",
    )(x2, w["wq"], w["wk"], w["wv"], w["wf"], w["bf"], place, ones)


def _decay_placement():
    place = np.zeros((DECAY_PARTS * LANES, 2 * LANES), np.float32)
    ones = np.zeros((1, 2 * LANES), np.float32)
    for h in range(FOX_HEADS):
        for t in range(DECAY_PARTS):
            place[t * LANES + h, DECAY_LANES * h + t] = 1.0
            place[t * LANES + h, LANES + DECAY_LANES * h + DECAY_PARTS + t] = -1.0
            ones[0, DECAY_LANES * h + DECAY_PARTS + t] = 1.0
            ones[0, LANES + DECAY_LANES * h + t] = 1.0
    return jnp.asarray(place, BF16), jnp.asarray(ones, F32)


def _flash_kernel(*refs, tile, seq, dqk, fox, pairs):
    if fox:
        q_ref, k_ref, v_ref, qx_ref, kx_ref, o_ref, s_ref, mc_ref, m_ref, acc_ref, mask_ref = refs
    else:
        q_ref, k_ref, v_ref, o_ref, s_ref, mc_ref, m_ref, acc_ref, mask_ref = refs
    nq = seq // tile
    n_steps = nq * (nq + 1) // 2
    first_pair = pl.program_id(1) * pairs
    lane = lax.broadcasted_iota(jnp.int32, (1, LANES), 1)
    lo = lane < HEAD_DIM
    reps = tile // LANES
    nt = (((1,), (1,)), ((), ()))
    lane_full = lax.broadcasted_iota(jnp.int32, (tile, LANES), 1)
    ones_lo = jnp.where(lane_full < HEAD_DIM, 1.0, 0.0).astype(BF16)
    ones_hi = jnp.where(lane_full < HEAD_DIM, 0.0, 1.0).astype(BF16)

    def q_operands(qi):
        q0 = pl.multiple_of(qi * tile, tile)
        qa = []
        for pp in range(pairs):
            q_pair = q_ref[0, pl.ds(q0, tile), pp * dqk:(pp + 1) * dqk]
            if fox:
                zero = jnp.zeros_like(q_pair)
                qx_blk = qx_ref[0, pl.ds(q0, tile), :]
                for hh in range(2):
                    qm = jnp.where(lo, q_pair, zero) if hh == 0 else jnp.where(lo, zero, q_pair)
                    head = 2 * (first_pair + pp) + hh
                    qx = jnp.where(lane // DECAY_LANES == head, qx_blk, jnp.zeros_like(qx_blk))
                    qa.append(jnp.concatenate([qm, qx], axis=1))
            else:
                qa += [q_pair[:, :LANES], q_pair[:, LANES:]]
        return qa

    def stage_a(qi, j):
        qa = q_operands(qi)
        k0 = pl.multiple_of(j * tile, tile)
        mask = mask_ref[jnp.where(j == qi, 1, 0)]
        for pp in range(pairs):
            k_pair = k_ref[0, pl.ds(k0, tile), pp * dqk:(pp + 1) * dqk]
            if fox:
                k_pair = jnp.concatenate([k_pair, kx_ref[0, pl.ds(k0, tile), :]], axis=1)
            for hh in range(2):
                h = 2 * pp + hh
                kh = k_pair if fox else k_pair[:, hh * LANES:(hh + 1) * LANES]
                s = lax.dot_general(qa[h], kh, nt, preferred_element_type=F32)
                s = s + mask
                s_ref[h] = s
                mc_ref[h] = jnp.broadcast_to(jnp.max(s, axis=1, keepdims=True), (tile, LANES))

    def stage_b(j):
        k0 = pl.multiple_of(j * tile, tile)
        m_cap = jnp.where(j == 0, NEG_INF, -NEG_INF)
        for pp in range(pairs):
            v_pair = v_ref[0, pl.ds(k0, tile), pp * LANES:(pp + 1) * LANES]
            zero = jnp.zeros_like(v_pair)
            v_stack = jnp.concatenate(
                [jnp.concatenate([jnp.where(lo, v_pair, zero), ones_lo], axis=1),
                 jnp.concatenate([jnp.where(lo, zero, v_pair), ones_hi], axis=1)],
                axis=0)
            ps = []
            alphas = []
            for hh in range(2):
                h = 2 * pp + hh
                m_prev = jnp.minimum(m_ref[h], m_cap)
                m_new = jnp.maximum(m_prev, mc_ref[h])
                alphas.append(jnp.exp2(m_prev - m_new))
                ps.append(jnp.exp2(s_ref[h] - jnp.concatenate([m_new] * reps, axis=1)).astype(BF16))
                m_ref[h] = m_new
            pv = jnp.dot(jnp.concatenate(ps, axis=1), v_stack, preferred_element_type=F32)
            alpha = jnp.where(lo, alphas[0], alphas[1])
            acc_ref[pp] = acc_ref[pp] * jnp.concatenate([alpha, alpha], axis=1) + pv

    def finalize(qi):
        q0 = pl.multiple_of(qi * tile, tile)
        for pp in range(pairs):
            a = acc_ref[pp]
            o_ref[0, pl.ds(q0, tile), pp * LANES:(pp + 1) * LANES] = (
                a[:, :LANES] / a[:, LANES:]).astype(o_ref.dtype)

    m_ref[...] = jnp.zeros(m_ref.shape, F32)
    acc_ref[...] = jnp.zeros(acc_ref.shape, F32)
    dcol = (lax.broadcasted_iota(jnp.int32, (tile, tile), 1)
            - lax.broadcasted_iota(jnp.int32, (tile, tile), 0))
    mask_ref[0] = jnp.zeros((tile, tile), F32)
    mask_ref[1] = jnp.where(dcol <= 0, 0.0, NEG_INF)
    stage_a(0, 0)

    def body(t, carry):
        qi, j = carry
        last = j == qi
        qn = jnp.where(last, qi + 1, qi)
        jn = jnp.where(last, 0, j + 1)
        stage_b(j)
        stage_a(qn, jn)

        @pl.when(last)
        def _():
            finalize(qi)

        return qn, jn

    lax.fori_loop(0, n_steps - 1, body, (jnp.int32(0), jnp.int32(0)))
    stage_b(nq - 1)
    finalize(nq - 1)


def _flash(q, k, v, qx=None, kx=None, *, tile):
    b, seq, _ = q.shape
    all_pairs = v.shape[2] // LANES
    dqk = q.shape[2] // all_pairs
    fox = qx is not None
    pairs = min(FLASH_PAIRS_PER_STEP, all_pairs)
    heads = 2 * pairs
    assert all_pairs % pairs == 0
    blk = lambda n: pl.BlockSpec((1, seq, pairs * n), lambda bi, g: (bi, 0, g))
    in_specs = [blk(dqk), blk(dqk), blk(LANES)]
    args = [q, k, v]
    if fox:
        in_specs += [pl.BlockSpec((1, seq, LANES), lambda bi, g: (bi, 0, 0))] * 2
        args += [qx, kx]
    scratch = [pltpu.VMEM((heads, tile, tile), F32), pltpu.VMEM((heads, tile, LANES), F32),
               pltpu.VMEM((heads, tile, LANES), F32), pltpu.VMEM((pairs, tile, 2 * LANES), F32),
               pltpu.VMEM((2, tile, tile), F32)]
    return pl.pallas_call(
        functools.partial(_flash_kernel, tile=tile, seq=seq, dqk=dqk, fox=fox, pairs=pairs),
        grid=(b, all_pairs // pairs),
        in_specs=in_specs,
        out_specs=blk(LANES),
        out_shape=jax.ShapeDtypeStruct((b, seq, all_pairs * LANES), BF16),
        scratch_shapes=scratch,
        compiler_params=_cparams(2),
        name="fox_attn" if fox else "mla_attn",
    )(*args)


def _swa_kernel(q_ref, k_ref, v_ref, bias_ref, sink_ref, o_ref, *, seq):
    nb = seq // BLOCK_Q
    lane = lax.broadcasted_iota(jnp.int32, (1, LANES), 1)
    lo = lane < HEAD_DIM
    group = SWA_HEADS // SWA_KV_HEADS
    band = 2 * BLOCK_Q
    ones_v = jnp.ones((band, LANES), BF16)
    nt = (((1,), (1,)), ((), ()))

    def block(n):
        r0 = pl.multiple_of(n * BLOCK_Q, BLOCK_Q)
        b0 = pl.multiple_of(jnp.maximum(n - 1, 0) * BLOCK_Q, BLOCK_Q)
        tab = jnp.where(n == 0, 1, 0)
        for pair in range(SWA_HEADS // 2):
            kvh = (2 * pair) // group
            ksl = slice(kvh * LANES, (kvh + 1) * LANES)
            q_pair = q_ref[0, pl.ds(r0, BLOCK_Q), pair * LANES:(pair + 1) * LANES]
            k_band = k_ref[0, pl.ds(b0, band), ksl]
            va = jnp.concatenate([v_ref[0, pl.ds(b0, band), ksl], ones_v], axis=1)
            zq = jnp.zeros_like(q_pair)
            res = []
            for hh in range(2):
                hd = 2 * pair + hh
                qh = jnp.where(lo, q_pair, zq) if hh == 0 else jnp.where(lo, zq, q_pair)
                s = lax.dot_general(qh, k_band, nt, preferred_element_type=F32) + bias_ref[tab, hd]
                sink = sink_ref[hd] * LOG2E
                mx = jnp.maximum(jnp.max(s, axis=1, keepdims=True), sink)
                p = jnp.exp2(s - mx).astype(BF16)
                pv = jnp.dot(p, va, preferred_element_type=F32)
                den = pv[:, LANES:] + jnp.exp2(sink - mx)
                res.append(pv[:, :LANES] / den)
            o_ref[0, pl.ds(r0, BLOCK_Q), pair * LANES:(pair + 1) * LANES] = (
                jnp.where(lo, res[0], res[1]).astype(o_ref.dtype))

    unroll = SWA_BLOCKS_PER_STEP if nb % SWA_BLOCKS_PER_STEP == 0 else 1

    def body(i, carry):
        for u in range(unroll):
            block(i * unroll + u)
        return carry

    lax.fori_loop(0, nb // unroll, body, 0)


def _swa(q, k, v, bias, sinks):
    b, seq, dq = q.shape
    dk = k.shape[2]
    return pl.pallas_call(
        functools.partial(_swa_kernel, seq=seq),
        grid=(b,),
        in_specs=[pl.BlockSpec((1, seq, dq), lambda bi: (bi, 0, 0)),
                  pl.BlockSpec((1, seq, dk), lambda bi: (bi, 0, 0)),
                  pl.BlockSpec((1, seq, dk), lambda bi: (bi, 0, 0)),
                  _const_spec(bias.shape),
                  pl.BlockSpec(memory_space=pltpu.SMEM)],
        out_specs=pl.BlockSpec((1, seq, dq), lambda bi: (bi, 0, 0)),
        out_shape=jax.ShapeDtypeStruct((b, seq, dq), BF16),
        compiler_params=_cparams(1),
        name="swa_attn",
    )(q, k, v, bias, sinks)


def _tail_kernel(*refs, n_parts, tf):
    o_refs = refs[:n_parts]
    w_refs = refs[n_parts:2 * n_parts]
    (x_ref, g1_ref, b1_ref, wu_ref, wd_ref, g2_ref, b2_ref, wg_ref, bg_ref, p_ref, wp_ref,
     y_ref) = refs[2 * n_parts:]
    tm = x_ref.shape[0]
    groups = [slice(r * (tm // TAIL_ROW_GROUPS), (r + 1) * (tm // TAIL_ROW_GROUPS))
              for r in range(TAIL_ROW_GROUPS)]
    mixes = []
    for rows in groups:
        mix = None
        for o_ref, w_ref in zip(o_refs, w_refs):
            d = jnp.dot(o_ref[rows, :], w_ref[...], preferred_element_type=F32)
            mix = d if mix is None else mix + d
        mixes.append(mix)
    x1s = [_layer_norm(DN_ALPHA * x_ref[rows, :] + mix, g1_ref[...], b1_ref[...])
           for rows, mix in zip(groups, mixes)]
    zs = []
    for x1 in x1s:
        xb = x1.astype(BF16)
        acc = None
        for f in range(D_FF // tf):
            hdn = jnp.maximum(
                jnp.dot(xb, wu_ref[:, f * tf:(f + 1) * tf], preferred_element_type=F32), 0.0)
            part = jnp.dot((hdn * hdn).astype(BF16), wd_ref[f * tf:(f + 1) * tf, :],
                           preferred_element_type=F32)
            acc = part if acc is None else acc + part
        zs.append(DN_ALPHA * x1 + acc)
    for rows, z in zip(groups, zs):
        y = _layer_norm(z, g2_ref[...], b2_ref[...])
        gate = jax.nn.sigmoid(
            jnp.dot(y.astype(BF16), wg_ref[...], preferred_element_type=F32) + bg_ref[...])
        emb = jnp.dot(p_ref[0, rows, :].astype(BF16), wp_ref[...], preferred_element_type=F32)
        y_ref[rows, :] = y + gate * emb


def _layer_tail(parts, w_outs, x2, g1, b1, w_up, w_down, g2, b2, w_gate, b_gate, p3, layer, w_proj,
                tm, tf):
    m = x2.shape[0]
    row = lambda c: pl.BlockSpec((tm, c), lambda i: (i, 0))
    p_spec = pl.BlockSpec((1, tm, D_PLE), lambda i: (layer, i, 0))
    vec = _const_spec((1, D_MODEL))
    return pl.pallas_call(
        functools.partial(_tail_kernel, n_parts=len(parts), tf=tf),
        grid=(m // tm,),
        in_specs=[row(o.shape[1]) for o in parts] + [_const_spec(w.shape) for w in w_outs]
        + [row(D_MODEL), vec, vec, _const_spec((D_MODEL, D_FF)), _const_spec((D_FF, D_MODEL)),
           vec, vec, _const_spec((D_MODEL, D_MODEL)), vec, p_spec,
           _const_spec((D_PLE, D_MODEL))],
        out_specs=row(D_MODEL),
        out_shape=jax.ShapeDtypeStruct((m, D_MODEL), F32),
        compiler_params=_cparams(1),
        name="layer_tail",
    )(*parts, *w_outs, x2, g1, b1, w_up, w_down, g2, b2, w_gate, b_gate, p3, w_proj)


def _pad_cols(w, n):
    return jnp.pad(w, ((0, 0), (0, n - w.shape[1])))


def _even_weights(w_in, q_norm, w_uq, kv_norm, w_ukv):
    sizes = [MLA_Q_LORA, MLA_KV_LORA, MLA_ROPE, SWA_HEADS * HEAD_DIM, SWA_KV_HEADS * HEAD_DIM]
    c_q, c_kv, k_r, q_s, k_s, v_s = jnp.split(w_in, np.cumsum(sizes).tolist(), axis=1)
    d = w_in.shape[0]
    zeros = lambda n: jnp.zeros((d, n), w_in.dtype)
    kr_blk = jnp.concatenate([zeros(MLA_NOPE), k_r, zeros(LANES - MLA_NOPE - MLA_ROPE)], axis=1)
    w_in2 = jnp.concatenate([c_q, c_kv, kr_blk, q_s * (HEAD_DIM ** -0.5 * LOG2E), k_s, v_s], axis=1)
    assert w_in2.shape[1] == _E_END
    r = w_uq.shape[0]
    uq = w_uq.reshape(r, MLA_HEADS, MLA_NOPE + MLA_ROPE)
    zq = jnp.zeros((r, MLA_HEADS, MLA_PAD - MLA_NOPE - MLA_ROPE), w_uq.dtype)
    uq_pad = jnp.concatenate([uq, zq], axis=-1).reshape(r, MLA_HEADS * MLA_PAD)
    rk = w_ukv.shape[0]
    ukv = w_ukv.reshape(rk, MLA_HEADS, MLA_NOPE + MLA_V)
    uk_pad = jnp.concatenate([ukv[..., :MLA_NOPE],
                              jnp.zeros((rk, MLA_HEADS, MLA_PAD - MLA_NOPE), w_ukv.dtype)],
                             axis=-1).reshape(rk, MLA_HEADS * MLA_PAD)
    uv = ukv[..., MLA_NOPE:].reshape(rk, MLA_HEADS * MLA_V)
    return dict(w_in=w_in2.astype(BF16), q_norm=q_norm.reshape(1, -1), kv_norm=kv_norm.reshape(1, -1),
                w_uq=uq_pad.astype(BF16), w_uk=uk_pad.astype(BF16), w_uv=uv.astype(BF16))


def _rope_tables(seq):
    inv = 1.0 / (ROPE_THETA ** (jnp.arange(0, MLA_ROPE, 2, dtype=F32) / MLA_ROPE))
    ang_t = inv[:, None] * jnp.arange(seq, dtype=F32)[None, :]
    cos = jnp.concatenate([jnp.cos(ang_t)] * 2, axis=0).T
    sin = jnp.sin(ang_t).T
    half = MLA_ROPE // 2
    zeros = lambda n: jnp.zeros((seq, n), F32)
    ones = jnp.ones((seq, MLA_NOPE), F32)
    tail = MLA_PAD - MLA_NOPE - MLA_ROPE
    kc = jnp.concatenate([zeros(MLA_NOPE), cos, zeros(tail)], axis=1)
    s_up = jnp.concatenate([zeros(MLA_NOPE + half), sin, zeros(tail)], axis=1)
    s_dn = jnp.concatenate([zeros(MLA_NOPE), -sin, zeros(half + tail)], axis=1)
    scale = (MLA_NOPE + MLA_ROPE) ** -0.5 * LOG2E
    qc = jnp.concatenate([ones, cos, zeros(tail)], axis=1) * scale
    return dict(qc=qc, qsu=s_up * scale, qsd=s_dn * scale, kc=kc, ksu=s_up, ksd=s_dn)


def _t5_bucket(dist):
    exact = REL_BUCKETS // 2
    d = jnp.maximum(dist, 1).astype(F32)
    large = exact + (jnp.log(d / exact) / math.log(REL_MAX_DIST / exact)
                     * (REL_BUCKETS - exact)).astype(jnp.int32)
    large = jnp.minimum(large, REL_BUCKETS - 1)
    return jnp.where(dist < exact, dist, large)


def _swa_bias_kernel(rel_ref, out_ref):
    a = lax.broadcasted_iota(jnp.int32, (BLOCK_Q, 2 * BLOCK_Q), 0)
    col = lax.broadcasted_iota(jnp.int32, (BLOCK_Q, 2 * BLOCK_Q), 1)
    for tab, shift in enumerate((BLOCK_Q, 0)):
        dist = a + shift - col
        valid = jnp.logical_and(dist >= 0, dist < SWA_WINDOW)
        bucket = _t5_bucket(jnp.maximum(dist, 0))
        for hd in range(SWA_HEADS):
            bias = jnp.zeros((BLOCK_Q, 2 * BLOCK_Q), F32)
            for bk in range(REL_BUCKETS):
                bias = jnp.where(bucket == bk, rel_ref[bk, hd], bias)
            out_ref[tab, hd] = jnp.where(valid, bias * LOG2E, NEG_INF)


def _swa_bias(rel_bias):
    return pl.pallas_call(
        _swa_bias_kernel,
        in_specs=[pl.BlockSpec(memory_space=pltpu.SMEM)],
        out_shape=jax.ShapeDtypeStruct((2, SWA_HEADS, BLOCK_Q, 2 * BLOCK_Q), F32),
        name="swa_bias_table",
    )(rel_bias)


def kernel(x, p, rel_bias, ev_w_in, ev_q_norm, ev_w_uq, ev_kv_norm, ev_w_ukv, ev_sinks, ev_w_out,
           od_w_in, od_b_f, od_w_out, ln1_g, ln1_b, w_up, w_down, ln2_g, ln2_b,
           ple_w_proj, ple_w_gate, ple_b_gate):
    b, seq, d = x.shape
    m = b * seq
    tm = min(512, seq)
    tm_proj = min(1024, seq)
    tile = min(512, seq)
    assert d == D_MODEL and seq % tm == 0 and seq % tile == 0 and seq % BLOCK_Q == 0
    assert seq >= 2 * BLOCK_Q
    tabs = _rope_tables(seq)
    bias = _swa_bias(rel_bias)
    x2 = x.reshape(m, d)
    p3 = p.reshape(p.shape[0], m, D_PLE)
    hd = FOX_HEADS * HEAD_DIM
    row2 = lambda v: v.reshape(1, -1)
    for i in range(DEPTH):
        j = i // 2
        if i % 2 == 0:
            w = _even_weights(ev_w_in[j], ev_q_norm[j], ev_w_uq[j], ev_kv_norm[j], ev_w_ukv[j])
            qm, km, vm, qsw, ksw, vsw = _even_proj(x2, w, tabs, seq, tm_proj)
            r3 = lambda t: t.reshape(b, seq, t.shape[1])
            o_mla = _flash(r3(qm), r3(km), r3(vm), tile=tile)
            o_swa = _swa(r3(qsw), r3(ksw), r3(vsw), bias, ev_sinks[j])
            w_out = ev_w_out[j].astype(BF16)
            n_mla = MLA_HEADS * MLA_V
            parts = [o_mla.reshape(m, -1), o_swa.reshape(m, -1)]
            w_outs = [w_out[:n_mla], w_out[n_mla:]]
        else:
            wi = od_w_in[j]
            w = dict(wq=(wi[:, :hd] * (HEAD_DIM ** -0.5 * LOG2E)).astype(BF16),
                     wk=wi[:, hd:2 * hd].astype(BF16),
                     wv=wi[:, 2 * hd:3 * hd].astype(BF16),
                     wf=_pad_cols(wi[:, 3 * hd:], LANES).astype(BF16),
                     bf=_pad_cols(row2(od_b_f[j]), LANES))
            q, k, v, qx, kx = _odd_proj(x2, w, seq, tm_proj)
            r3 = lambda t: t.reshape(b, seq, t.shape[1])
            parts = [_flash(r3(q), r3(k), r3(v), r3(qx), r3(kx), tile=tile).reshape(m, -1)]
            w_outs = [od_w_out[j].astype(BF16)]
        x2 = _layer_tail(parts, w_outs, x2, row2(ln1_g[i]), row2(ln1_b[i]),
                         w_up[i].astype(BF16), w_down[i].astype(BF16), row2(ln2_g[i]), row2(ln2_b[i]),
                         ple_w_gate[i].astype(BF16), row2(ple_b_gate[i]), p3, i,
                         ple_w_proj[i].astype(BF16), tm, min(1024, D_FF))
    return x2.reshape(b, seq, d)
```

```python
import functools
import math

import jax
import jax.numpy as jnp
import numpy as np
from jax import lax
from jax.experimental import pallas as pl
from jax.experimental.pallas import tpu as pltpu

D_MODEL = 1024
HEAD_DIM = 64
MLA_HEADS = 8
MLA_NOPE = 64
MLA_ROPE = 32
MLA_V = 64
MLA_Q_LORA = 384
MLA_KV_LORA = 256
ROPE_THETA = 10000.0
SWA_HEADS = 8
SWA_KV_HEADS = 2
SWA_WINDOW = 128
REL_BUCKETS = 32
REL_MAX_DIST = 128
FOX_HEADS = 16
D_FF = 4 * D_MODEL
D_PLE = 256
BLOCK_Q = 128
DEPTH = 4
DN_ALPHA = (2 * DEPTH) ** 0.25
NORM_EPS = 1e-5
NEG_INF = -1e30

LANES = 128
MLA_PAD = 128
VMEM_LIMIT = 52 * 1024 * 1024
LOG2E = math.log2(math.e)
FLASH_PAIRS_PER_STEP = 4
TAIL_ROW_GROUPS = 2
SWA_BLOCKS_PER_STEP = 8
DECAY_PARTS = 3
DECAY_LANES = 8

F32 = jnp.float32
BF16 = jnp.bfloat16

_E_CQ = 0
_E_CKV = _E_CQ + MLA_Q_LORA
_E_KR = _E_CKV + MLA_KV_LORA
_E_QS = _E_KR + LANES
_E_KS = _E_QS + SWA_HEADS * HEAD_DIM
_E_VS = _E_KS + SWA_KV_HEADS * HEAD_DIM
_E_END = _E_VS + SWA_KV_HEADS * HEAD_DIM
assert SWA_KV_HEADS * HEAD_DIM == LANES


def _cparams(n_axes):
    return pltpu.CompilerParams(dimension_semantics=("arbitrary",) * n_axes,
                                vmem_limit_bytes=VMEM_LIMIT)


def _const_spec(shape):
    return pl.BlockSpec(shape, lambda *_: (0,) * len(shape), pipeline_mode=pl.Buffered(1))


def _layer_norm(y, g, b):
    mu = jnp.mean(y, axis=-1, keepdims=True)
    yc = y - mu
    var = jnp.mean(yc * yc, axis=-1, keepdims=True)
    return yc * lax.rsqrt(var + NORM_EPS) * g + b


def _rms_norm(y, g):
    return y * lax.rsqrt(jnp.mean(y * y, axis=-1, keepdims=True) + NORM_EPS) * g


def _rope_block(y, c_ref, s_up_ref, s_dn_ref):
    half = MLA_ROPE // 2
    return (y * c_ref[...] + pltpu.roll(y, half, 1) * s_up_ref[...]
            + pltpu.roll(y, LANES - half, 1) * s_dn_ref[...])


def _even_proj_kernel(x_ref, w_in_ref, qn_ref, kvn_ref, w_uq_ref, w_uk_ref, w_uv_ref,
                      qc_ref, qsu_ref, qsd_ref, kc_ref, ksu_ref, ksd_ref,
                      qm_ref, km_ref, vm_ref, qsw_ref, ksw_ref, vsw_ref):
    lo = lax.broadcasted_iota(jnp.int32, (1, LANES), 1) < HEAD_DIM
    xb = x_ref[...].astype(BF16)
    h = jnp.dot(xb, w_in_ref[:, :_E_QS], preferred_element_type=F32)
    cq = _rms_norm(h[:, _E_CQ:_E_CKV], qn_ref[...]).astype(BF16)
    ckv = _rms_norm(h[:, _E_CKV:_E_KR], kvn_ref[...]).astype(BF16)
    qa = jnp.dot(cq, w_uq_ref[...], preferred_element_type=F32)
    for hd in range(MLA_HEADS):
        sl = slice(hd * MLA_PAD, (hd + 1) * MLA_PAD)
        qm_ref[:, sl] = _rope_block(qa[:, sl], qc_ref, qsu_ref, qsd_ref).astype(BF16)
    hs = jnp.dot(xb, w_in_ref[:, _E_QS:], preferred_element_type=F32)
    kr = _rope_block(h[:, _E_KR:_E_QS], kc_ref, ksu_ref, ksd_ref)
    kn = jnp.dot(ckv, w_uk_ref[...], preferred_element_type=F32)
    for hd in range(MLA_HEADS):
        sl = slice(hd * MLA_PAD, (hd + 1) * MLA_PAD)
        km_ref[:, sl] = (kn[:, sl] + kr).astype(BF16)
    vm_ref[...] = jnp.dot(ckv, w_uv_ref[...], preferred_element_type=F32).astype(BF16)
    n_q = _E_KS - _E_QS
    qsw_ref[...] = hs[:, :n_q].astype(BF16)
    for src, dst_ref in ((hs[:, n_q:n_q + LANES], ksw_ref), (hs[:, n_q + LANES:], vsw_ref)):
        swapped = pltpu.roll(src, HEAD_DIM, 1)
        dst_ref[:, :LANES] = jnp.where(lo, src, swapped).astype(BF16)
        dst_ref[:, LANES:] = jnp.where(lo, swapped, src).astype(BF16)


def _even_proj(x2, w, tabs, seq, tm):
    m = x2.shape[0]
    n_pos = seq // tm
    row = lambda n: pl.BlockSpec((tm, n), lambda i: (i, 0))
    pos = lambda n: pl.BlockSpec((tm, n), lambda i: (i % n_pos, 0))
    hq = MLA_HEADS * MLA_PAD
    out_shape = (
        jax.ShapeDtypeStruct((m, hq), BF16), jax.ShapeDtypeStruct((m, hq), BF16),
        jax.ShapeDtypeStruct((m, MLA_HEADS * MLA_V), BF16),
        jax.ShapeDtypeStruct((m, SWA_HEADS * HEAD_DIM), BF16),
        jax.ShapeDtypeStruct((m, 2 * SWA_KV_HEADS * HEAD_DIM), BF16),
        jax.ShapeDtypeStruct((m, 2 * SWA_KV_HEADS * HEAD_DIM), BF16),
    )
    return pl.pallas_call(
        _even_proj_kernel,
        grid=(m // tm,),
        in_specs=[row(D_MODEL), _const_spec(w["w_in"].shape), _const_spec((1, MLA_Q_LORA)),
                  _const_spec((1, MLA_KV_LORA)), _const_spec(w["w_uq"].shape),
                  _const_spec(w["w_uk"].shape), _const_spec(w["w_uv"].shape)] + [pos(LANES)] * 6,
        out_specs=(row(hq), row(hq), row(MLA_HEADS * MLA_V), row(SWA_HEADS * HEAD_DIM),
                   row(2 * SWA_KV_HEADS * HEAD_DIM), row(2 * SWA_KV_HEADS * HEAD_DIM)),
        out_shape=out_shape,
        compiler_params=_cparams(1),
        name="even_proj",
    )(x2, w["w_in"], w["q_norm"], w["kv_norm"], w["w_uq"], w["w_uk"], w["w_uv"],
      tabs["qc"], tabs["qsu"], tabs["qsd"], tabs["kc"], tabs["ksu"], tabs["ksd"])


def _split3(c):
    hi = c.astype(BF16)
    r1 = c - hi.astype(F32)
    mid = r1.astype(BF16)
    lo = (r1 - mid.astype(F32)).astype(BF16)
    return hi, mid, lo


def _odd_proj_kernel(x_ref, wq_ref, wk_ref, wv_ref, wf_ref, bf_ref, place_ref, ones_ref,
                     q_ref, k_ref, v_ref, qx_ref, kx_ref, carry_ref, *, n_pos):
    tm = x_ref.shape[0]

    @pl.when(pl.program_id(0) % n_pos == 0)
    def _():
        carry_ref[...] = jnp.zeros_like(carry_ref)

    xb = x_ref[...].astype(BF16)
    z = jnp.dot(xb, wf_ref[...], preferred_element_type=F32) + bf_ref[...]
    logf = jnp.minimum(z, 0.0) - jnp.log1p(jnp.exp(-jnp.abs(z)))
    q_ref[...] = jnp.dot(xb, wq_ref[...], preferred_element_type=F32).astype(BF16)
    k_ref[...] = jnp.dot(xb, wk_ref[...], preferred_element_type=F32).astype(BF16)
    v_ref[...] = jnp.dot(xb, wv_ref[...], preferred_element_type=F32).astype(BF16)
    rows = lax.broadcasted_iota(jnp.int32, (tm, LANES), 0)
    c = logf
    sh = 1
    while sh < tm:
        c = c + jnp.where(rows >= sh, pltpu.roll(c, sh, 0), 0.0)
        sh *= 2
    c = c + carry_ref[0:1, :]
    carry_ref[...] = jnp.broadcast_to(c[tm - 1:tm, :], carry_ref.shape)
    parts = jnp.concatenate(_split3(c * LOG2E), axis=1)
    placed = jnp.dot(parts, place_ref[...], preferred_element_type=F32) + ones_ref[...]
    qx_ref[...] = placed[:, :LANES].astype(BF16)
    kx_ref[...] = placed[:, LANES:].astype(BF16)


def _odd_proj(x2, w, seq, tm):
    m = x2.shape[0]
    n_pos = seq // tm
    row = lambda n: pl.BlockSpec((tm, n), lambda i: (i, 0))
    hd = FOX_HEADS * HEAD_DIM
    out_shape = (jax.ShapeDtypeStruct((m, hd), BF16),) * 3 + (jax.ShapeDtypeStruct((m, LANES), BF16),) * 2
    place, ones = _decay_placement()
    return pl.pallas_call(
        functools.partial(_odd_proj_kernel, n_pos=n_pos),
        grid=(m // tm,),
        in_specs=[row(D_MODEL), _const_spec((D_MODEL, hd)), _const_spec((D_MODEL, hd)),
                  _const_spec((D_MODEL, hd)), _const_spec((D_MODEL, LANES)), _const_spec((1, LANES)),
                  _const_spec(place.shape), _const_spec(ones.shape)],
        out_specs=(row(hd), row(hd), row(hd), row(LANES), row(LANES)),
        out_shape=out_shape,
        scratch_shapes=[pltpu.VMEM((8, LANES), F32)],
        compiler_params=_cparams(1),
        name="odd_proj",
    )(x2, w["wq"], w["wk"], w["wv"], w["wf"], w["bf"], place, ones)


def _decay_placement():
    place = np.zeros((DECAY_PARTS * LANES, 2 * LANES), np.float32)
    ones = np.zeros((1, 2 * LANES), np.float32)
    for h in range(FOX_HEADS):
        for t in range(DECAY_PARTS):
            place[t * LANES + h, DECAY_LANES * h + t] = 1.0
            place[t * LANES + h, LANES + DECAY_LANES * h + DECAY_PARTS + t] = -1.0
            ones[0, DECAY_LANES * h + DECAY_PARTS + t] = 1.0
            ones[0, LANES + DECAY_LANES * h + t] = 1.0
    return jnp.asarray(place, BF16), jnp.asarray(ones, F32)


def _flash_kernel(*refs, tile, seq, dqk, fox, pairs):
    if fox:
        q_ref, k_ref, v_ref, qx_ref, kx_ref, o_ref, s_ref, mc_ref, m_ref, acc_ref, mask_ref = refs
    else:
        q_ref, k_ref, v_ref, o_ref, s_ref, mc_ref, m_ref, acc_ref, mask_ref = refs
    nq = seq // tile
    n_steps = nq * (nq + 1) // 2
    first_pair = pl.program_id(1) * pairs
    lane = lax.broadcasted_iota(jnp.int32, (1, LANES), 1)
    lo = lane < HEAD_DIM
    reps = tile // LANES
    nt = (((1,), (1,)), ((), ()))
    lane_full = lax.broadcasted_iota(jnp.int32, (tile, LANES), 1)
    ones_lo = jnp.where(lane_full < HEAD_DIM, 1.0, 0.0).astype(BF16)
    ones_hi = jnp.where(lane_full < HEAD_DIM, 0.0, 1.0).astype(BF16)

    def q_operands(qi):
        q0 = pl.multiple_of(qi * tile, tile)
        qa = []
        for pp in range(pairs):
            q_pair = q_ref[0, pl.ds(q0, tile), pp * dqk:(pp + 1) * dqk]
            if fox:
                zero = jnp.zeros_like(q_pair)
                qx_blk = qx_ref[0, pl.ds(q0, tile), :]
                for hh in range(2):
                    qm = jnp.where(lo, q_pair, zero) if hh == 0 else jnp.where(lo, zero, q_pair)
                    head = 2 * (first_pair + pp) + hh
                    qx = jnp.where(lane // DECAY_LANES == head, qx_blk, jnp.zeros_like(qx_blk))
                    qa.append(jnp.concatenate([qm, qx], axis=1))
            else:
                qa += [q_pair[:, :LANES], q_pair[:, LANES:]]
        return qa

    def stage_a(qi, j):
        qa = q_operands(qi)
        k0 = pl.multiple_of(j * tile, tile)
        mask = mask_ref[jnp.where(j == qi, 1, 0)]
        for pp in range(pairs):
            k_pair = k_ref[0, pl.ds(k0, tile), pp * dqk:(pp + 1) * dqk]
            if fox:
                k_pair = jnp.concatenate([k_pair, kx_ref[0, pl.ds(k0, tile), :]], axis=1)
            for hh in range(2):
                h = 2 * pp + hh
                kh = k_pair if fox else k_pair[:, hh * LANES:(hh + 1) * LANES]
                s = lax.dot_general(qa[h], kh, nt, preferred_element_type=F32)
                s = s + mask
                s_ref[h] = s
                mc_ref[h] = jnp.broadcast_to(jnp.max(s, axis=1, keepdims=True), (tile, LANES))

    def stage_b(j):
        k0 = pl.multiple_of(j * tile, tile)
        m_cap = jnp.where(j == 0, NEG_INF, -NEG_INF)
        for pp in range(pairs):
            v_pair = v_ref[0, pl.ds(k0, tile), pp * LANES:(pp + 1) * LANES]
            zero = jnp.zeros_like(v_pair)
            v_stack = jnp.concatenate(
                [jnp.concatenate([jnp.where(lo, v_pair, zero), ones_lo], axis=1),
                 jnp.concatenate([jnp.where(lo, zero, v_pair), ones_hi], axis=1)],
                axis=0)
            ps = []
            alphas = []
            for hh in range(2):
                h = 2 * pp + hh
                m_prev = jnp.minimum(m_ref[h], m_cap)
                m_new = jnp.maximum(m_prev, mc_ref[h])
                alphas.append(jnp.exp2(m_prev - m_new))
                ps.append(jnp.exp2(s_ref[h] - jnp.concatenate([m_new] * reps, axis=1)).astype(BF16))
                m_ref[h] = m_new
            pv = jnp.dot(jnp.concatenate(ps, axis=1), v_stack, preferred_element_type=F32)
            alpha = jnp.where(lo, alphas[0], alphas[1])
            acc_ref[pp] = acc_ref[pp] * jnp.concatenate([alpha, alpha], axis=1) + pv

    def finalize(qi):
        q0 = pl.multiple_of(qi * tile, tile)
        for pp in range(pairs):
            a = acc_ref[pp]
            o_ref[0, pl.ds(q0, tile), pp * LANES:(pp + 1) * LANES] = (
                a[:, :LANES] / a[:, LANES:]).astype(o_ref.dtype)

    @pl.when(jnp.logical_and(pl.program_id(0) == 0, pl.program_id(1) == 0))
    def _():
        m_ref[...] = jnp.zeros(m_ref.shape, F32)
        acc_ref[...] = jnp.zeros(acc_ref.shape, F32)
        dcol = (lax.broadcasted_iota(jnp.int32, (tile, tile), 1)
                - lax.broadcasted_iota(jnp.int32, (tile, tile), 0))
        mask_ref[0] = jnp.zeros((tile, tile), F32)
        mask_ref[1] = jnp.where(dcol <= 0, 0.0, NEG_INF)

    stage_a(0, 0)

    def body(t, carry):
        qi, j = carry
        last = j == qi
        qn = jnp.where(last, qi + 1, qi)
        jn = jnp.where(last, 0, j + 1)
        stage_b(j)
        stage_a(qn, jn)

        @pl.when(last)
        def _():
            finalize(qi)

        return qn, jn

    lax.fori_loop(0, n_steps - 1, body, (jnp.int32(0), jnp.int32(0)))
    stage_b(nq - 1)
    finalize(nq - 1)


def _flash(q, k, v, qx=None, kx=None, *, tile):
    b, seq, _ = q.shape
    all_pairs = v.shape[2] // LANES
    dqk = q.shape[2] // all_pairs
    fox = qx is not None
    pairs = min(FLASH_PAIRS_PER_STEP, all_pairs)
    heads = 2 * pairs
    assert all_pairs % pairs == 0
    blk = lambda n: pl.BlockSpec((1, seq, pairs * n), lambda bi, g: (bi, 0, g))
    in_specs = [blk(dqk), blk(dqk), blk(LANES)]
    args = [q, k, v]
    if fox:
        in_specs += [pl.BlockSpec((1, seq, LANES), lambda bi, g: (bi, 0, 0))] * 2
        args += [qx, kx]
    scratch = [pltpu.VMEM((heads, tile, tile), F32), pltpu.VMEM((heads, tile, LANES), F32),
               pltpu.VMEM((heads, tile, LANES), F32), pltpu.VMEM((pairs, tile, 2 * LANES), F32),
               pltpu.VMEM((2, tile, tile), F32)]
    return pl.pallas_call(
        functools.partial(_flash_kernel, tile=tile, seq=seq, dqk=dqk, fox=fox, pairs=pairs),
        grid=(b, all_pairs // pairs),
        in_specs=in_specs,
        out_specs=blk(LANES),
        out_shape=jax.ShapeDtypeStruct((b, seq, all_pairs * LANES), BF16),
        scratch_shapes=scratch,
        compiler_params=_cparams(2),
        name="fox_attn" if fox else "mla_attn",
    )(*args)


def _swa_kernel(q_ref, k_ref, v_ref, bias_ref, sink_ref, o_ref, *, seq):
    nb = seq // BLOCK_Q
    lane = lax.broadcasted_iota(jnp.int32, (1, LANES), 1)
    lo = lane < HEAD_DIM
    group = SWA_HEADS // SWA_KV_HEADS
    band = 2 * BLOCK_Q
    ones_v = jnp.ones((band, LANES), BF16)
    nt = (((1,), (1,)), ((), ()))

    def block(n):
        r0 = pl.multiple_of(n * BLOCK_Q, BLOCK_Q)
        b0 = pl.multiple_of(jnp.maximum(n - 1, 0) * BLOCK_Q, BLOCK_Q)
        tab = jnp.where(n == 0, 1, 0)
        for pair in range(SWA_HEADS // 2):
            kvh = (2 * pair) // group
            ksl = slice(kvh * LANES, (kvh + 1) * LANES)
            q_pair = q_ref[0, pl.ds(r0, BLOCK_Q), pair * LANES:(pair + 1) * LANES]
            k_band = k_ref[0, pl.ds(b0, band), ksl]
            va = jnp.concatenate([v_ref[0, pl.ds(b0, band), ksl], ones_v], axis=1)
            zq = jnp.zeros_like(q_pair)
            res = []
            for hh in range(2):
                hd = 2 * pair + hh
                qh = jnp.where(lo, q_pair, zq) if hh == 0 else jnp.where(lo, zq, q_pair)
                s = lax.dot_general(qh, k_band, nt, preferred_element_type=F32) + bias_ref[tab, hd]
                sink = sink_ref[hd] * LOG2E
                mx = jnp.maximum(jnp.max(s, axis=1, keepdims=True), sink)
                p = jnp.exp2(s - mx).astype(BF16)
                pv = jnp.dot(p, va, preferred_element_type=F32)
                den = pv[:, LANES:] + jnp.exp2(sink - mx)
                res.append(pv[:, :LANES] / den)
            o_ref[0, pl.ds(r0, BLOCK_Q), pair * LANES:(pair + 1) * LANES] = (
                jnp.where(lo, res[0], res[1]).astype(o_ref.dtype))

    unroll = SWA_BLOCKS_PER_STEP if nb % SWA_BLOCKS_PER_STEP == 0 else 1

    def body(i, carry):
        for u in range(unroll):
            block(i * unroll + u)
        return carry

    lax.fori_loop(0, nb // unroll, body, 0)


def _swa(q, k, v, bias, sinks):
    b, seq, dq = q.shape
    dk = k.shape[2]
    return pl.pallas_call(
        functools.partial(_swa_kernel, seq=seq),
        grid=(b,),
        in_specs=[pl.BlockSpec((1, seq, dq), lambda bi: (bi, 0, 0)),
                  pl.BlockSpec((1, seq, dk), lambda bi: (bi, 0, 0)),
                  pl.BlockSpec((1, seq, dk), lambda bi: (bi, 0, 0)),
                  _const_spec(bias.shape),
                  pl.BlockSpec(memory_space=pltpu.SMEM)],
        out_specs=pl.BlockSpec((1, seq, dq), lambda bi: (bi, 0, 0)),
        out_shape=jax.ShapeDtypeStruct((b, seq, dq), BF16),
        compiler_params=_cparams(1),
        name="swa_attn",
    )(q, k, v, bias, sinks)


def _tail_kernel(*refs, n_parts, tf):
    o_refs = refs[:n_parts]
    w_refs = refs[n_parts:2 * n_parts]
    (x_ref, g1_ref, b1_ref, wu_ref, wd_ref, g2_ref, b2_ref, wg_ref, bg_ref, p_ref, wp_ref,
     y_ref) = refs[2 * n_parts:]
    tm = x_ref.shape[0]
    groups = [slice(r * (tm // TAIL_ROW_GROUPS), (r + 1) * (tm // TAIL_ROW_GROUPS))
              for r in range(TAIL_ROW_GROUPS)]
    mixes = []
    for rows in groups:
        mix = None
        for o_ref, w_ref in zip(o_refs, w_refs):
            d = jnp.dot(o_ref[rows, :], w_ref[...], preferred_element_type=F32)
            mix = d if mix is None else mix + d
        mixes.append(mix)
    x1s = [_layer_norm(DN_ALPHA * x_ref[rows, :] + mix, g1_ref[...], b1_ref[...])
           for rows, mix in zip(groups, mixes)]
    zs = []
    for x1 in x1s:
        xb = x1.astype(BF16)
        acc = None
        for f in range(D_FF // tf):
            hdn = jnp.maximum(
                jnp.dot(xb, wu_ref[:, f * tf:(f + 1) * tf], preferred_element_type=F32), 0.0)
            part = jnp.dot((hdn * hdn).astype(BF16), wd_ref[f * tf:(f + 1) * tf, :],
                           preferred_element_type=F32)
            acc = part if acc is None else acc + part
        zs.append(DN_ALPHA * x1 + acc)
    for rows, z in zip(groups, zs):
        y = _layer_norm(z, g2_ref[...], b2_ref[...])
        gate = jax.nn.sigmoid(
            jnp.dot(y.astype(BF16), wg_ref[...], preferred_element_type=F32) + bg_ref[...])
        emb = jnp.dot(p_ref[0, rows, :].astype(BF16), wp_ref[...], preferred_element_type=F32)
        y_ref[rows, :] = y + gate * emb


def _layer_tail(parts, w_outs, x2, g1, b1, w_up, w_down, g2, b2, w_gate, b_gate, p3, layer, w_proj,
                tm, tf):
    m = x2.shape[0]
    row = lambda c: pl.BlockSpec((tm, c), lambda i: (i, 0))
    p_spec = pl.BlockSpec((1, tm, D_PLE), lambda i: (layer, i, 0))
    vec = _const_spec((1, D_MODEL))
    return pl.pallas_call(
        functools.partial(_tail_kernel, n_parts=len(parts), tf=tf),
        grid=(m // tm,),
        in_specs=[row(o.shape[1]) for o in parts] + [_const_spec(w.shape) for w in w_outs]
        + [row(D_MODEL), vec, vec, _const_spec((D_MODEL, D_FF)), _const_spec((D_FF, D_MODEL)),
           vec, vec, _const_spec((D_MODEL, D_MODEL)), vec, p_spec,
           _const_spec((D_PLE, D_MODEL))],
        out_specs=row(D_MODEL),
        out_shape=jax.ShapeDtypeStruct((m, D_MODEL), F32),
        compiler_params=_cparams(1),
        name="layer_tail",
    )(*parts, *w_outs, x2, g1, b1, w_up, w_down, g2, b2, w_gate, b_gate, p3, w_proj)


def _pad_cols(w, n):
    return jnp.pad(w, ((0, 0), (0, n - w.shape[1])))


def _even_weights(w_in, q_norm, w_uq, kv_norm, w_ukv):
    sizes = [MLA_Q_LORA, MLA_KV_LORA, MLA_ROPE, SWA_HEADS * HEAD_DIM, SWA_KV_HEADS * HEAD_DIM]
    c_q, c_kv, k_r, q_s, k_s, v_s = jnp.split(w_in, np.cumsum(sizes).tolist(), axis=1)
    d = w_in.shape[0]
    zeros = lambda n: jnp.zeros((d, n), w_in.dtype)
    kr_blk = jnp.concatenate([zeros(MLA_NOPE), k_r, zeros(LANES - MLA_NOPE - MLA_ROPE)], axis=1)
    w_in2 = jnp.concatenate([c_q, c_kv, kr_blk, q_s * (HEAD_DIM ** -0.5 * LOG2E), k_s, v_s], axis=1)
    assert w_in2.shape[1] == _E_END
    r = w_uq.shape[0]
    uq = w_uq.reshape(r, MLA_HEADS, MLA_NOPE + MLA_ROPE)
    zq = jnp.zeros((r, MLA_HEADS, MLA_PAD - MLA_NOPE - MLA_ROPE), w_uq.dtype)
    uq_pad = jnp.concatenate([uq, zq], axis=-1).reshape(r, MLA_HEADS * MLA_PAD)
    rk = w_ukv.shape[0]
    ukv = w_ukv.reshape(rk, MLA_HEADS, MLA_NOPE + MLA_V)
    uk_pad = jnp.concatenate([ukv[..., :MLA_NOPE],
                              jnp.zeros((rk, MLA_HEADS, MLA_PAD - MLA_NOPE), w_ukv.dtype)],
                             axis=-1).reshape(rk, MLA_HEADS * MLA_PAD)
    uv = ukv[..., MLA_NOPE:].reshape(rk, MLA_HEADS * MLA_V)
    return dict(w_in=w_in2.astype(BF16), q_norm=q_norm.reshape(1, -1), kv_norm=kv_norm.reshape(1, -1),
                w_uq=uq_pad.astype(BF16), w_uk=uk_pad.astype(BF16), w_uv=uv.astype(BF16))


def _rope_tables(seq):
    inv = 1.0 / (ROPE_THETA ** (jnp.arange(0, MLA_ROPE, 2, dtype=F32) / MLA_ROPE))
    ang_t = inv[:, None] * jnp.arange(seq, dtype=F32)[None, :]
    cos = jnp.concatenate([jnp.cos(ang_t)] * 2, axis=0).T
    sin = jnp.sin(ang_t).T
    half = MLA_ROPE // 2
    zeros = lambda n: jnp.zeros((seq, n), F32)
    ones = jnp.ones((seq, MLA_NOPE), F32)
    tail = MLA_PAD - MLA_NOPE - MLA_ROPE
    kc = jnp.concatenate([zeros(MLA_NOPE), cos, zeros(tail)], axis=1)
    s_up = jnp.concatenate([zeros(MLA_NOPE + half), sin, zeros(tail)], axis=1)
    s_dn = jnp.concatenate([zeros(MLA_NOPE), -sin, zeros(half + tail)], axis=1)
    scale = (MLA_NOPE + MLA_ROPE) ** -0.5 * LOG2E
    qc = jnp.concatenate([ones, cos, zeros(tail)], axis=1) * scale
    return dict(qc=qc, qsu=s_up * scale, qsd=s_dn * scale, kc=kc, ksu=s_up, ksd=s_dn)


def _t5_bucket(dist):
    exact = REL_BUCKETS // 2
    d = jnp.maximum(dist, 1).astype(F32)
    large = exact + (jnp.log(d / exact) / math.log(REL_MAX_DIST / exact)
                     * (REL_BUCKETS - exact)).astype(jnp.int32)
    large = jnp.minimum(large, REL_BUCKETS - 1)
    return jnp.where(dist < exact, dist, large)


def _swa_bias_kernel(rel_ref, out_ref):
    a = lax.broadcasted_iota(jnp.int32, (BLOCK_Q, 2 * BLOCK_Q), 0)
    col = lax.broadcasted_iota(jnp.int32, (BLOCK_Q, 2 * BLOCK_Q), 1)
    for tab, shift in enumerate((BLOCK_Q, 0)):
        dist = a + shift - col
        valid = jnp.logical_and(dist >= 0, dist < SWA_WINDOW)
        bucket = _t5_bucket(jnp.maximum(dist, 0))
        for hd in range(SWA_HEADS):
            bias = jnp.zeros((BLOCK_Q, 2 * BLOCK_Q), F32)
            for bk in range(REL_BUCKETS):
                bias = jnp.where(bucket == bk, rel_ref[bk, hd], bias)
            out_ref[tab, hd] = jnp.where(valid, bias * LOG2E, NEG_INF)


def _swa_bias(rel_bias):
    return pl.pallas_call(
        _swa_bias_kernel,
        in_specs=[pl.BlockSpec(memory_space=pltpu.SMEM)],
        out_shape=jax.ShapeDtypeStruct((2, SWA_HEADS, BLOCK_Q, 2 * BLOCK_Q), F32),
        name="swa_bias_table",
    )(rel_bias)


def kernel(x, p, rel_bias, ev_w_in, ev_q_norm, ev_w_uq, ev_kv_norm, ev_w_ukv, ev_sinks, ev_w_out,
           od_w_in, od_b_f, od_w_out, ln1_g, ln1_b, w_up, w_down, ln2_g, ln2_b,
           ple_w_proj, ple_w_gate, ple_b_gate):
    b, seq, d = x.shape
    m = b * seq
    tm = min(512, seq)
    tm_proj = min(1024, seq)
    tile = min(512, seq)
    assert d == D_MODEL and seq % tm == 0 and seq % tile == 0 and seq % BLOCK_Q == 0
    assert seq >= 2 * BLOCK_Q
    tabs = _rope_tables(seq)
    bias = _swa_bias(rel_bias)
    x2 = x.reshape(m, d)
    p3 = p.reshape(p.shape[0], m, D_PLE)
    hd = FOX_HEADS * HEAD_DIM
    row2 = lambda v: v.reshape(1, -1)
    for i in range(DEPTH):
        j = i // 2
        if i % 2 == 0:
            w = _even_weights(ev_w_in[j], ev_q_norm[j], ev_w_uq[j], ev_kv_norm[j], ev_w_ukv[j])
            qm, km, vm, qsw, ksw, vsw = _even_proj(x2, w, tabs, seq, tm_proj)
            r3 = lambda t: t.reshape(b, seq, t.shape[1])
            o_mla = _flash(r3(qm), r3(km), r3(vm), tile=tile)
            o_swa = _swa(r3(qsw), r3(ksw), r3(vsw), bias, ev_sinks[j])
            w_out = ev_w_out[j].astype(BF16)
            n_mla = MLA_HEADS * MLA_V
            parts = [o_mla.reshape(m, -1), o_swa.reshape(m, -1)]
            w_outs = [w_out[:n_mla], w_out[n_mla:]]
        else:
            wi = od_w_in[j]
            w = dict(wq=(wi[:, :hd] * (HEAD_DIM ** -0.5 * LOG2E)).astype(BF16),
                     wk=wi[:, hd:2 * hd].astype(BF16),
                     wv=wi[:, 2 * hd:3 * hd].astype(BF16),
                     wf=_pad_cols(wi[:, 3 * hd:], LANES).astype(BF16),
                     bf=_pad_cols(row2(od_b_f[j]), LANES))
            q, k, v, qx, kx = _odd_proj(x2, w, seq, tm_proj)
            r3 = lambda t: t.reshape(b, seq, t.shape[1])
            parts = [_flash(r3(q), r3(k), r3(v), r3(qx), r3(kx), tile=tile).reshape(m, -1)]
            w_outs = [od_w_out[j].astype(BF16)]
        x2 = _layer_tail(parts, w_outs, x2, row2(ln1_g[i]), row2(ln1_b[i]),
                         w_up[i].astype(BF16), w_down[i].astype(BF16), row2(ln2_g[i]), row2(ln2_b[i]),
                         ple_w_gate[i].astype(BF16), row2(ple_b_gate[i]), p3, i,
                         ple_w_proj[i].astype(BF16), tm, min(1024, D_FF))
    return x2.reshape(b, seq, d)
```
